```python
import jax, jax.numpy as jnp
from jax import lax
import numpy as np

D_MODEL = 2048
BATCH = 8
SEQ = 8192
DEPTH = 2

D_MIX = D_MODEL
ATT_HEAD_DIM = 128
D_ATT = D_MIX // 2
ATT_HEADS = D_ATT // ATT_HEAD_DIM
D_RNN = D_MIX // 4
RNN_HEADS = 4
RNN_HEAD_DIM = D_RNN // RNN_HEADS
POOL_WINDOWS = (2, 4, 8, 16)
D_POOL = D_MIX - D_ATT - D_RNN
POOL_GROUPS = len(POOL_WINDOWS)
POOL_GROUP_DIM = D_POOL // POOL_GROUPS
D_IN_PROJ = 3 * D_ATT + 2 * D_RNN + D_POOL
CONV_WIDTH = 4
RG_C = 8.0
D_FF = 5632
Q_BLOCK = 128
NORM_EPS = 1e-6

kernel_name = "hybrid_stickbreak_rglru_pool_macaron"


def rms_norm(x, g):
    xf = x.astype(jnp.float32)
    y = xf * lax.rsqrt(jnp.mean(xf * xf, axis=-1, keepdims=True) + NORM_EPS)
    return (y * g.astype(jnp.float32)).astype(x.dtype)


def swiglu(h, w_gate, w_up, w_down):
    return (jax.nn.silu(h @ w_gate) * (h @ w_up)) @ w_down


def stick_breaking_attention(q, k, v):
    B, S, H, dh = q.shape
    scale = dh ** -0.5
    outs = []
    for blk in range(S // Q_BLOCK):
        start, end = blk * Q_BLOCK, (blk + 1) * Q_BLOCK
        qb = q[:, start:end]
        kb, vb = k[:, :end], v[:, :end]
        z = jnp.einsum('bthd,bshd->bhts', qb, kb,
                       preferred_element_type=jnp.float32) * scale
        t_pos = start + jnp.arange(Q_BLOCK, dtype=jnp.int32)
        s_pos = jnp.arange(end, dtype=jnp.int32)
        causal = s_pos[None, :] < t_pos[:, None]
        log_fail = jnp.where(causal, jax.nn.log_sigmoid(-z), 0.0)
        after = lax.cumsum(log_fail, axis=3, reverse=True) - log_fail
        w = jnp.where(causal, jnp.exp(jax.nn.log_sigmoid(z) + after), 0.0)
        outs.append(jnp.einsum('bhts,bshd->bthd', w.astype(vb.dtype), vb))
    return jnp.concatenate(outs, axis=1)


def causal_depthwise_conv(x, w, b):
    C = x.shape[-1]
    y = lax.conv_general_dilated(
        x, w[:, None, :], window_strides=(1,), padding=[(CONV_WIDTH - 1, 0)],
        dimension_numbers=('NWC', 'WIO', 'NWC'), feature_group_count=C)
    return y + b


def _scan_combine(c1, c2):
    a1, b1 = c1
    a2, b2 = c2
    return a1 * a2, a2 * b1 + b2


def rg_lru_mixer(xg, xr, conv_w, conv_b, w_a, b_a, w_x, b_x, lam):
    B, S, _ = xr.shape
    u = causal_depthwise_conv(xr, conv_w, conv_b)
    uh = u.reshape(B, S, RNN_HEADS, RNN_HEAD_DIM)
    r = jax.nn.sigmoid(jnp.einsum('bshi,hij->bshj', uh, w_a).reshape(B, S, D_RNN) + b_a)
    i = jax.nn.sigmoid(jnp.einsum('bshi,hij->bshj', uh, w_x).reshape(B, S, D_RNN) + b_x)
    log_a = RG_C * r.astype(jnp.float32) * jax.nn.log_sigmoid(lam.astype(jnp.float32))
    a = jnp.exp(log_a)
    b = jnp.sqrt(-jnp.expm1(2.0 * log_a)) * (i * u).astype(jnp.float32)
    _, h = lax.associative_scan(_scan_combine, (a, b), axis=1)
    return jax.nn.gelu(xg) * h.astype(xg.dtype)


def pool_mixer(xp, w_pool, scale):
    B, S, _ = xp.shape
    xg = xp.reshape(B, S, POOL_GROUPS, POOL_GROUP_DIM).astype(jnp.float32)
    cs = jnp.cumsum(xg, axis=1)
    t = jnp.arange(S, dtype=jnp.int32)
    outs = []
    for g, win in enumerate(POOL_WINDOWS):
        c = cs[:, :, g]
        lower = jnp.pad(c, ((0, 0), (win, 0), (0, 0)))[:, :S]
        count = jnp.minimum(t + 1, win).astype(jnp.float32)[None, :, None]
        outs.append((c - lower) / count - xg[:, :, g])
    d = jnp.stack(outs, axis=2).astype(xp.dtype)
    y = jnp.einsum('bsgi,gij->bsgj', d, w_pool).reshape(B, S, D_POOL)
    return y * scale


def _fwd_setup_inputs(seed: int = 0) -> dict:
    key = jax.random.key(seed)
    ks = jax.random.split(key, 24)
    f32 = jnp.float32

    def nrm(k, shape, fan_in):
        return jax.random.normal(k, shape, f32) * (fan_in ** -0.5)

    def gain(k, shape):
        return 1.0 + 0.02 * jax.random.normal(k, shape, f32)

    def bias(k, shape):
        return 0.02 * jax.random.normal(k, shape, f32)

    L = DEPTH
    out_scale = (2.0 * DEPTH) ** -0.5
    u = jax.random.uniform(ks[13], (L, D_RNN), f32, 0.9, 0.999)
    s = u ** (1.0 / RG_C)
    rg_lambda = jnp.log(s) - jnp.log1p(-s)
    return {
        "x": jax.random.normal(ks[0], (BATCH, SEQ, D_MODEL), f32),
        "norm_ffn1": gain(ks[1], (L, D_MODEL)),
        "ffn1_gate": nrm(ks[2], (L, D_MODEL, D_FF), D_MODEL),
        "ffn1_up": nrm(ks[3], (L, D_MODEL, D_FF), D_MODEL),
        "ffn1_down": nrm(ks[4], (L, D_FF, D_MODEL), D_FF) * out_scale,
        "norm_mix": gain(ks[5], (L, D_MODEL)),
        "w_in": nrm(ks[6], (L, D_MODEL, D_IN_PROJ), D_MODEL),
        "conv_w": nrm(ks[7], (L, CONV_WIDTH, D_RNN), CONV_WIDTH),
        "conv_b": bias(ks[8], (L, D_RNN)),
        "rg_w_a": nrm(ks[9], (L, RNN_HEADS, RNN_HEAD_DIM, RNN_HEAD_DIM), RNN_HEAD_DIM),
        "rg_b_a": bias(ks[10], (L, D_RNN)),
        "rg_w_x": nrm(ks[11], (L, RNN_HEADS, RNN_HEAD_DIM, RNN_HEAD_DIM), RNN_HEAD_DIM),
        "rg_b_x": bias(ks[12], (L, D_RNN)),
        "rg_lambda": rg_lambda,
        "pool_w": nrm(ks[14], (L, POOL_GROUPS, POOL_GROUP_DIM, POOL_GROUP_DIM), POOL_GROUP_DIM),
        "pool_scale": gain(ks[15], (L, D_POOL)),
        "w_out": nrm(ks[16], (L, D_MIX, D_MODEL), D_MIX) * out_scale,
        "norm_ffn2": gain(ks[17], (L, D_MODEL)),
        "ffn2_gate": nrm(ks[18], (L, D_MODEL, D_FF), D_MODEL),
        "ffn2_up": nrm(ks[19], (L, D_MODEL, D_FF), D_MODEL),
        "ffn2_down": nrm(ks[20], (L, D_FF, D_MODEL), D_FF) * out_scale,
        "norm_final": gain(ks[21], (D_MODEL,)),
    }


def _fwd_reference(x, norm_ffn1, ffn1_gate, ffn1_up, ffn1_down, norm_mix, w_in, conv_w, conv_b,
              rg_w_a, rg_b_a, rg_w_x, rg_b_x, rg_lambda, pool_w, pool_scale, w_out,
              norm_ffn2, ffn2_gate, ffn2_up, ffn2_down, norm_final):
    B, S, _ = x.shape
    splits = [int(v) for v in np.cumsum([D_ATT, D_ATT, D_ATT, D_RNN, D_RNN])]
    for l in range(DEPTH):
        x = x + 0.5 * swiglu(rms_norm(x, norm_ffn1[l]), ffn1_gate[l], ffn1_up[l], ffn1_down[l])
        h = rms_norm(x, norm_mix[l])
        z = h @ w_in[l]
        q, k, v, rg_gate, rg_x, pool_in = jnp.split(z, splits, axis=-1)
        att = stick_breaking_attention(
            q.reshape(B, S, ATT_HEADS, ATT_HEAD_DIM),
            k.reshape(B, S, ATT_HEADS, ATT_HEAD_DIM),
            v.reshape(B, S, ATT_HEADS, ATT_HEAD_DIM)).reshape(B, S, D_ATT)
        rnn = rg_lru_mixer(rg_gate, rg_x, conv_w[l], conv_b[l], rg_w_a[l], rg_b_a[l],
                           rg_w_x[l], rg_b_x[l], rg_lambda[l])
        pool = pool_mixer(pool_in, pool_w[l], pool_scale[l])
        x = x + jnp.concatenate([att, rnn, pool], axis=-1) @ w_out[l]
        x = x + 0.5 * swiglu(rms_norm(x, norm_ffn2[l]), ffn2_gate[l], ffn2_up[l], ffn2_down[l])
    return rms_norm(x, norm_final)


import jax as _jax
import jax.numpy as _jnp

TWIN_FORMAT = 'train_step'
FWD_PARAMS = ['x', 'norm_ffn1', 'ffn1_gate', 'ffn1_up', 'ffn1_down', 'norm_mix', 'w_in', 'conv_w', 'conv_b', 'rg_w_a', 'rg_b_a', 'rg_w_x', 'rg_b_x', 'rg_lambda', 'pool_w', 'pool_scale', 'w_out', 'norm_ffn2', 'ffn2_gate', 'ffn2_up', 'ffn2_down', 'norm_final']
TWIN_WEIGHTS = ['norm_ffn1', 'ffn1_gate', 'ffn1_up', 'ffn1_down', 'norm_mix', 'w_in', 'conv_w', 'conv_b', 'rg_w_a', 'rg_b_a', 'rg_w_x', 'rg_b_x', 'rg_lambda', 'pool_w', 'pool_scale', 'w_out', 'norm_ffn2', 'ffn2_gate', 'ffn2_up', 'ffn2_down', 'norm_final']
TWIN_DIFF_INPUT = 'x'
TWIN_INPUTS = ['x', 'norm_ffn1', 'ffn1_gate', 'ffn1_up', 'ffn1_down', 'norm_mix', 'w_in', 'conv_w', 'conv_b', 'rg_w_a', 'rg_b_a', 'rg_w_x', 'rg_b_x', 'rg_lambda', 'pool_w', 'pool_scale', 'w_out', 'norm_ffn2', 'ffn2_gate', 'ffn2_up', 'ffn2_down', 'norm_final', 'loss_target', 'm_norm_ffn1', 'm_ffn1_gate', 'm_ffn1_up', 'm_ffn1_down', 'm_norm_mix', 'm_w_in', 'm_conv_w', 'm_conv_b', 'm_rg_w_a', 'm_rg_b_a', 'm_rg_w_x', 'm_rg_b_x', 'm_rg_lambda', 'm_pool_w', 'm_pool_scale', 'm_w_out', 'm_norm_ffn2', 'm_ffn2_gate', 'm_ffn2_up', 'm_ffn2_down', 'm_norm_final', 'v_norm_ffn1', 'v_ffn1_gate', 'v_ffn1_up', 'v_ffn1_down', 'v_norm_mix', 'v_w_in', 'v_conv_w', 'v_conv_b', 'v_rg_w_a', 'v_rg_b_a', 'v_rg_w_x', 'v_rg_b_x', 'v_rg_lambda', 'v_pool_w', 'v_pool_scale', 'v_w_out', 'v_norm_ffn2', 'v_ffn2_gate', 'v_ffn2_up', 'v_ffn2_down', 'v_norm_final']
TWIN_OUTPUTS = ['loss', 'grad_x', 'grad_norm_ffn1', 'grad_ffn1_gate', 'grad_ffn1_up', 'grad_ffn1_down', 'grad_norm_mix', 'grad_w_in', 'grad_conv_w', 'grad_conv_b', 'grad_rg_w_a', 'grad_rg_b_a', 'grad_rg_w_x', 'grad_rg_b_x', 'grad_rg_lambda', 'grad_pool_w', 'grad_pool_scale', 'grad_w_out', 'grad_norm_ffn2', 'grad_ffn2_gate', 'grad_ffn2_up', 'grad_ffn2_down', 'grad_norm_final', 'delta_norm_ffn1', 'delta_ffn1_gate', 'delta_ffn1_up', 'delta_ffn1_down', 'delta_norm_mix', 'delta_w_in', 'delta_conv_w', 'delta_conv_b', 'delta_rg_w_a', 'delta_rg_b_a', 'delta_rg_w_x', 'delta_rg_b_x', 'delta_rg_lambda', 'delta_pool_w', 'delta_pool_scale', 'delta_w_out', 'delta_norm_ffn2', 'delta_ffn2_gate', 'delta_ffn2_up', 'delta_ffn2_down', 'delta_norm_final', 'new_m_norm_ffn1', 'new_m_ffn1_gate', 'new_m_ffn1_up', 'new_m_ffn1_down', 'new_m_norm_mix', 'new_m_w_in', 'new_m_conv_w', 'new_m_conv_b', 'new_m_rg_w_a', 'new_m_rg_b_a', 'new_m_rg_w_x', 'new_m_rg_b_x', 'new_m_rg_lambda', 'new_m_pool_w', 'new_m_pool_scale', 'new_m_w_out', 'new_m_norm_ffn2', 'new_m_ffn2_gate', 'new_m_ffn2_up', 'new_m_ffn2_down', 'new_m_norm_final', 'new_v_norm_ffn1', 'new_v_ffn1_gate', 'new_v_ffn1_up', 'new_v_ffn1_down', 'new_v_norm_mix', 'new_v_w_in', 'new_v_conv_w', 'new_v_conv_b', 'new_v_rg_w_a', 'new_v_rg_b_a', 'new_v_rg_w_x', 'new_v_rg_b_x', 'new_v_rg_lambda', 'new_v_pool_w', 'new_v_pool_scale', 'new_v_w_out', 'new_v_norm_ffn2', 'new_v_ffn2_gate', 'new_v_ffn2_up', 'new_v_ffn2_down', 'new_v_norm_final']
TWIN_LEAF_KINDS = {'loss': 'loss', 'grad_x': 'grad_x', 'grad_norm_ffn1': 'grad_w', 'grad_ffn1_gate': 'grad_w', 'grad_ffn1_up': 'grad_w', 'grad_ffn1_down': 'grad_w', 'grad_norm_mix': 'grad_w', 'grad_w_in': 'grad_w', 'grad_conv_w': 'grad_w', 'grad_conv_b': 'grad_w', 'grad_rg_w_a': 'grad_w', 'grad_rg_b_a': 'grad_w', 'grad_rg_w_x': 'grad_w', 'grad_rg_b_x': 'grad_w', 'grad_rg_lambda': 'grad_w', 'grad_pool_w': 'grad_w', 'grad_pool_scale': 'grad_w', 'grad_w_out': 'grad_w', 'grad_norm_ffn2': 'grad_w', 'grad_ffn2_gate': 'grad_w', 'grad_ffn2_up': 'grad_w', 'grad_ffn2_down': 'grad_w', 'grad_norm_final': 'grad_w', 'delta_norm_ffn1': 'delta_w', 'delta_ffn1_gate': 'delta_w', 'delta_ffn1_up': 'delta_w', 'delta_ffn1_down': 'delta_w', 'delta_norm_mix': 'delta_w', 'delta_w_in': 'delta_w', 'delta_conv_w': 'delta_w', 'delta_conv_b': 'delta_w', 'delta_rg_w_a': 'delta_w', 'delta_rg_b_a': 'delta_w', 'delta_rg_w_x': 'delta_w', 'delta_rg_b_x': 'delta_w', 'delta_rg_lambda': 'delta_w', 'delta_pool_w': 'delta_w', 'delta_pool_scale': 'delta_w', 'delta_w_out': 'delta_w', 'delta_norm_ffn2': 'delta_w', 'delta_ffn2_gate': 'delta_w', 'delta_ffn2_up': 'delta_w', 'delta_ffn2_down': 'delta_w', 'delta_norm_final': 'delta_w', 'new_m_norm_ffn1': 'new_m', 'new_m_ffn1_gate': 'new_m', 'new_m_ffn1_up': 'new_m', 'new_m_ffn1_down': 'new_m', 'new_m_norm_mix': 'new_m', 'new_m_w_in': 'new_m', 'new_m_conv_w': 'new_m', 'new_m_conv_b': 'new_m', 'new_m_rg_w_a': 'new_m', 'new_m_rg_b_a': 'new_m', 'new_m_rg_w_x': 'new_m', 'new_m_rg_b_x': 'new_m', 'new_m_rg_lambda': 'new_m', 'new_m_pool_w': 'new_m', 'new_m_pool_scale': 'new_m', 'new_m_w_out': 'new_m', 'new_m_norm_ffn2': 'new_m', 'new_m_ffn2_gate': 'new_m', 'new_m_ffn2_up': 'new_m', 'new_m_ffn2_down': 'new_m', 'new_m_norm_final': 'new_m', 'new_v_norm_ffn1': 'new_v', 'new_v_ffn1_gate': 'new_v', 'new_v_ffn1_up': 'new_v', 'new_v_ffn1_down': 'new_v', 'new_v_norm_mix': 'new_v', 'new_v_w_in': 'new_v', 'new_v_conv_w': 'new_v', 'new_v_conv_b': 'new_v', 'new_v_rg_w_a': 'new_v', 'new_v_rg_b_a': 'new_v', 'new_v_rg_w_x': 'new_v', 'new_v_rg_b_x': 'new_v', 'new_v_rg_lambda': 'new_v', 'new_v_pool_w': 'new_v', 'new_v_pool_scale': 'new_v', 'new_v_w_out': 'new_v', 'new_v_norm_ffn2': 'new_v', 'new_v_ffn2_gate': 'new_v', 'new_v_ffn2_up': 'new_v', 'new_v_ffn2_down': 'new_v', 'new_v_norm_final': 'new_v'}


def _forward(args):
    return _fwd_reference(*[args[k] for k in FWD_PARAMS])


def _output_shape():
    def fwd():
        inp = _fwd_setup_inputs(0)
        return _fwd_reference(*[inp[k] for k in FWD_PARAMS])
    out = _jax.eval_shape(fwd)
    return out.shape, out.dtype

N_MICROBATCH = 1
ADAM_LR = 0.001
ADAM_B1 = 0.9
ADAM_B2 = 0.999
ADAM_EPS = 1e-08
ADAM_WD = 0.01
ADAM_STEP = 10
PER_EXAMPLE_BATCH_AXIS = {'x': 0, 'loss_target': 0}
SHARED_INPUTS = []
_WEIGHT_DTYPES = {'norm_ffn1': _jnp.float32, 'ffn1_gate': _jnp.float32, 'ffn1_up': _jnp.float32, 'ffn1_down': _jnp.float32, 'norm_mix': _jnp.float32, 'w_in': _jnp.float32, 'conv_w': _jnp.float32, 'conv_b': _jnp.float32, 'rg_w_a': _jnp.float32, 'rg_b_a': _jnp.float32, 'rg_w_x': _jnp.float32, 'rg_b_x': _jnp.float32, 'rg_lambda': _jnp.float32, 'pool_w': _jnp.float32, 'pool_scale': _jnp.float32, 'w_out': _jnp.float32, 'norm_ffn2': _jnp.float32, 'ffn2_gate': _jnp.float32, 'ffn2_up': _jnp.float32, 'ffn2_down': _jnp.float32, 'norm_final': _jnp.float32}
MOMENT_SCALE = {'norm_ffn1': 2.726615e-02, 'ffn1_gate': 1.160006e-02, 'ffn1_up': 1.123598e-02, 'ffn1_down': 3.726154e-02, 'norm_mix': 4.395705e-02, 'w_in': 2.999627e-02, 'conv_w': 2.796484e-02, 'conv_b': 3.327727e-01, 'rg_w_a': 8.822286e-03, 'rg_b_a': 8.548187e-03, 'rg_w_x': 1.606769e-02, 'rg_b_x': 1.014252e-02, 'rg_lambda': 1.508495e-02, 'pool_w': 5.142889e-02, 'pool_scale': 5.508717e-02, 'w_out': 7.911838e-02, 'norm_ffn2': 2.495288e-02, 'ffn2_gate': 1.070081e-02, 'ffn2_up': 1.037152e-02, 'ffn2_down': 3.436480e-02, 'norm_final': 3.198848e+01}


def _to_microbatches(a, axis):
    t = _jnp.moveaxis(a, axis, 0)
    t = t.reshape((N_MICROBATCH, t.shape[0] // N_MICROBATCH) + t.shape[1:])
    return _jnp.moveaxis(t, 1, axis + 1)


def setup_inputs(seed: int = 0) -> dict:
    inp = _fwd_setup_inputs(seed)
    key = _jax.random.fold_in(_jax.random.key(seed), 7919)
    shape, _ = _output_shape()
    out = dict(inp)
    out["loss_target"] = _jax.random.normal(_jax.random.fold_in(key, 0), shape, _jnp.float32)
    for i, name in enumerate(TWIN_WEIGHTS):
        w = inp[name].astype(_jnp.float32)
        if MOMENT_SCALE is None:
            s = _jnp.sqrt(_jnp.mean(_jnp.square(w)) + 1e-30)
        else:
            s = MOMENT_SCALE[name]
        km, kv = _jax.random.split(_jax.random.fold_in(key, i + 1))
        out[name] = w
        out["m_" + name] = s * _jax.random.normal(km, w.shape, _jnp.float32)
        out["v_" + name] = (s * s) * _jax.random.uniform(kv, w.shape, _jnp.float32, 0.5, 1.5)
    if N_MICROBATCH > 1:
        for name, axis in PER_EXAMPLE_BATCH_AXIS.items():
            out[name] = _to_microbatches(out[name], axis)
    return {'x': out['x'], 'norm_ffn1': out['norm_ffn1'], 'ffn1_gate': out['ffn1_gate'], 'ffn1_up': out['ffn1_up'], 'ffn1_down': out['ffn1_down'], 'norm_mix': out['norm_mix'], 'w_in': out['w_in'], 'conv_w': out['conv_w'], 'conv_b': out['conv_b'], 'rg_w_a': out['rg_w_a'], 'rg_b_a': out['rg_b_a'], 'rg_w_x': out['rg_w_x'], 'rg_b_x': out['rg_b_x'], 'rg_lambda': out['rg_lambda'], 'pool_w': out['pool_w'], 'pool_scale': out['pool_scale'], 'w_out': out['w_out'], 'norm_ffn2': out['norm_ffn2'], 'ffn2_gate': out['ffn2_gate'], 'ffn2_up': out['ffn2_up'], 'ffn2_down': out['ffn2_down'], 'norm_final': out['norm_final'], 'loss_target': out['loss_target'], 'm_norm_ffn1': out['m_norm_ffn1'], 'm_ffn1_gate': out['m_ffn1_gate'], 'm_ffn1_up': out['m_ffn1_up'], 'm_ffn1_down': out['m_ffn1_down'], 'm_norm_mix': out['m_norm_mix'], 'm_w_in': out['m_w_in'], 'm_conv_w': out['m_conv_w'], 'm_conv_b': out['m_conv_b'], 'm_rg_w_a': out['m_rg_w_a'], 'm_rg_b_a': out['m_rg_b_a'], 'm_rg_w_x': out['m_rg_w_x'], 'm_rg_b_x': out['m_rg_b_x'], 'm_rg_lambda': out['m_rg_lambda'], 'm_pool_w': out['m_pool_w'], 'm_pool_scale': out['m_pool_scale'], 'm_w_out': out['m_w_out'], 'm_norm_ffn2': out['m_norm_ffn2'], 'm_ffn2_gate': out['m_ffn2_gate'], 'm_ffn2_up': out['m_ffn2_up'], 'm_ffn2_down': out['m_ffn2_down'], 'm_norm_final': out['m_norm_final'], 'v_norm_ffn1': out['v_norm_ffn1'], 'v_ffn1_gate': out['v_ffn1_gate'], 'v_ffn1_up': out['v_ffn1_up'], 'v_ffn1_down': out['v_ffn1_down'], 'v_norm_mix': out['v_norm_mix'], 'v_w_in': out['v_w_in'], 'v_conv_w': out['v_conv_w'], 'v_conv_b': out['v_conv_b'], 'v_rg_w_a': out['v_rg_w_a'], 'v_rg_b_a': out['v_rg_b_a'], 'v_rg_w_x': out['v_rg_w_x'], 'v_rg_b_x': out['v_rg_b_x'], 'v_rg_lambda': out['v_rg_lambda'], 'v_pool_w': out['v_pool_w'], 'v_pool_scale': out['v_pool_scale'], 'v_w_out': out['v_w_out'], 'v_norm_ffn2': out['v_norm_ffn2'], 'v_ffn2_gate': out['v_ffn2_gate'], 'v_ffn2_up': out['v_ffn2_up'], 'v_ffn2_down': out['v_ffn2_down'], 'v_norm_final': out['v_norm_final']}


def _loss(weights, diff, rest, loss_target):
    with _jax.named_scope("forward"):
        args = {**rest, TWIN_DIFF_INPUT: diff, **{k: w.astype(_WEIGHT_DTYPES[k]) for k, w in weights.items()}}
        y = _forward(args)
    with _jax.named_scope("loss_head"):
        err = _jnp.square(y.astype(_jnp.float32) - loss_target)
        return 0.5 * _jnp.sum(_jnp.mean(err, axis=-1)) if err.ndim else 0.5 * err


def _adamw(w, g, m, v):
    m = ADAM_B1 * m + (1.0 - ADAM_B1) * g
    v = ADAM_B2 * v + (1.0 - ADAM_B2) * _jnp.square(g)
    m_hat = m / (1.0 - ADAM_B1 ** ADAM_STEP)
    v_hat = v / (1.0 - ADAM_B2 ** ADAM_STEP)
    delta = -ADAM_LR * (m_hat / (_jnp.sqrt(v_hat) + ADAM_EPS) + ADAM_WD * w)
    return delta, m, v


def reference(x, norm_ffn1, ffn1_gate, ffn1_up, ffn1_down, norm_mix, w_in, conv_w, conv_b, rg_w_a, rg_b_a, rg_w_x, rg_b_x, rg_lambda, pool_w, pool_scale, w_out, norm_ffn2, ffn2_gate, ffn2_up, ffn2_down, norm_final, loss_target, m_norm_ffn1, m_ffn1_gate, m_ffn1_up, m_ffn1_down, m_norm_mix, m_w_in, m_conv_w, m_conv_b, m_rg_w_a, m_rg_b_a, m_rg_w_x, m_rg_b_x, m_rg_lambda, m_pool_w, m_pool_scale, m_w_out, m_norm_ffn2, m_ffn2_gate, m_ffn2_up, m_ffn2_down, m_norm_final, v_norm_ffn1, v_ffn1_gate, v_ffn1_up, v_ffn1_down, v_norm_mix, v_w_in, v_conv_w, v_conv_b, v_rg_w_a, v_rg_b_a, v_rg_w_x, v_rg_b_x, v_rg_lambda, v_pool_w, v_pool_scale, v_w_out, v_norm_ffn2, v_ffn2_gate, v_ffn2_up, v_ffn2_down, v_norm_final):
    given = dict(x=x, norm_ffn1=norm_ffn1, ffn1_gate=ffn1_gate, ffn1_up=ffn1_up, ffn1_down=ffn1_down, norm_mix=norm_mix, w_in=w_in, conv_w=conv_w, conv_b=conv_b, rg_w_a=rg_w_a, rg_b_a=rg_b_a, rg_w_x=rg_w_x, rg_b_x=rg_b_x, rg_lambda=rg_lambda, pool_w=pool_w, pool_scale=pool_scale, w_out=w_out, norm_ffn2=norm_ffn2, ffn2_gate=ffn2_gate, ffn2_up=ffn2_up, ffn2_down=ffn2_down, norm_final=norm_final, loss_target=loss_target, m_norm_ffn1=m_norm_ffn1, m_ffn1_gate=m_ffn1_gate, m_ffn1_up=m_ffn1_up, m_ffn1_down=m_ffn1_down, m_norm_mix=m_norm_mix, m_w_in=m_w_in, m_conv_w=m_conv_w, m_conv_b=m_conv_b, m_rg_w_a=m_rg_w_a, m_rg_b_a=m_rg_b_a, m_rg_w_x=m_rg_w_x, m_rg_b_x=m_rg_b_x, m_rg_lambda=m_rg_lambda, m_pool_w=m_pool_w, m_pool_scale=m_pool_scale, m_w_out=m_w_out, m_norm_ffn2=m_norm_ffn2, m_ffn2_gate=m_ffn2_gate, m_ffn2_up=m_ffn2_up, m_ffn2_down=m_ffn2_down, m_norm_final=m_norm_final, v_norm_ffn1=v_norm_ffn1, v_ffn1_gate=v_ffn1_gate, v_ffn1_up=v_ffn1_up, v_ffn1_down=v_ffn1_down, v_norm_mix=v_norm_mix, v_w_in=v_w_in, v_conv_w=v_conv_w, v_conv_b=v_conv_b, v_rg_w_a=v_rg_w_a, v_rg_b_a=v_rg_b_a, v_rg_w_x=v_rg_w_x, v_rg_b_x=v_rg_b_x, v_rg_lambda=v_rg_lambda, v_pool_w=v_pool_w, v_pool_scale=v_pool_scale, v_w_out=v_w_out, v_norm_ffn2=v_norm_ffn2, v_ffn2_gate=v_ffn2_gate, v_ffn2_up=v_ffn2_up, v_ffn2_down=v_ffn2_down, v_norm_final=v_norm_final)
    weights = {n: given[n] for n in TWIN_WEIGHTS}
    shared = {n: given[n] for n in SHARED_INPUTS}
    per_example = {n: given[n] for n in ['x']}
    grad_fn = _jax.value_and_grad(_loss, argnums=(0, 1))

    def one_microbatch(ex, loss_target):
        ex = dict(ex)
        diff = ex.pop(TWIN_DIFF_INPUT)
        return grad_fn(weights, diff, {**shared, **ex}, loss_target)

    if N_MICROBATCH == 1:
        loss, (grad_w, grad_x) = one_microbatch(per_example, given["loss_target"])
    else:
        def body(carry, xs):
            loss_sum, grad_sum = carry
            l_k, (gw_k, gx_k) = one_microbatch(xs[0], xs[1])
            with _jax.named_scope("update"):
                return (loss_sum + l_k, _jax.tree.map(_jnp.add, grad_sum, gw_k)), gx_k

        init = (_jnp.zeros((), _jnp.float32), _jax.tree.map(_jnp.zeros_like, weights))
        (loss, grad_w), grad_x = _jax.lax.scan(body, init, (per_example, given["loss_target"]))
    with _jax.named_scope("update"):
        delta_w, new_m, new_v = {}, {}, {}
        for n in TWIN_WEIGHTS:
            delta_w[n], new_m[n], new_v[n] = _adamw(weights[n], grad_w[n], given["m_" + n], given["v_" + n])
    return (loss, grad_x, *[grad_w[n] for n in TWIN_WEIGHTS], *[delta_w[n] for n in TWIN_WEIGHTS],
            *[new_m[n] for n in TWIN_WEIGHTS], *[new_v[n] for n in TWIN_WEIGHTS])
```

```python
import functools

import jax
import jax.numpy as jnp
from jax import lax
from jax.experimental import pallas as pl
from jax.experimental.pallas import tpu as pltpu

F32 = jnp.float32
BF16 = jnp.bfloat16
MESH = pl.DeviceIdType.MESH

HEAD = 128
POOL_WINDOWS = (2, 4, 8, 16)
CONV_WIDTH = 4
RG_C = 8.0
NORM_EPS = 1e-6
N_CHIP = 4
N_DEV = 8
ADAM_LR, ADAM_B1, ADAM_B2, ADAM_EPS, ADAM_WD, ADAM_STEP = 0.001, 0.9, 0.999, 1e-08, 0.01, 10
VMEM_LIMIT_V7X = 56 * 2 ** 20

WEIGHTS = ['norm_ffn1', 'ffn1_gate', 'ffn1_up', 'ffn1_down', 'norm_mix', 'w_in', 'conv_w', 'conv_b', 'rg_w_a',
           'rg_b_a', 'rg_w_x', 'rg_b_x', 'rg_lambda', 'pool_w', 'pool_scale', 'w_out', 'norm_ffn2', 'ffn2_gate',
           'ffn2_up', 'ffn2_down', 'norm_final']
BIG = ['ffn1_gate', 'ffn1_up', 'ffn1_down', 'w_in', 'w_out', 'ffn2_gate', 'ffn2_up', 'ffn2_down']
SMALL = [n for n in WEIGHTS if n not in BIG]


def _params(*sem):
    return pltpu.CompilerParams(dimension_semantics=sem, vmem_limit_bytes=VMEM_LIMIT_V7X)


def _tile(n, pref):
    t = min(n, pref)
    while n % t:
        t //= 2
    return t


def _dot(a, b):
    return jnp.dot(a, b, preferred_element_type=F32)


def _dot_nt(a, b):
    return lax.dot_general(a, b, (((1,), (1,)), ((), ())), preferred_element_type=F32)


def _dot_tn(a, b):
    return lax.dot_general(a, b, (((0,), (0,)), ((), ())), preferred_element_type=F32)


def _sigmoid(x):
    return 1.0 / (1.0 + jnp.exp(-x))


def _gelu(x):
    return 0.5 * x * (1.0 + jnp.tanh(0.7978845608028654 * (x + 0.044715 * x * x * x)))


def _expm1(x):
    series = x * (1.0 + x * 0.5 * (1.0 + x * (1.0 / 3.0) * (1.0 + x * 0.25 * (1.0 + x * 0.2))))
    return jnp.where(jnp.abs(x) < 0.1, series, jnp.exp(x) - 1.0)


def _neg_softplus(z):
    return -(jnp.maximum(z, 0.0) + jnp.log(1.0 + jnp.exp(-jnp.abs(z))))


def _split_bf16(x):
    hi = x.astype(BF16)
    return hi, (x - hi.astype(F32)).astype(BF16)


def _wspec(k, n, l, jpos):
    return pl.BlockSpec((None, None, k, n), lambda *g: (g[jpos], l, 0, 0))


def rmsnorm_fwd(x, g):
    S, D = x.shape
    tm = _tile(S, 512)

    def body(x_ref, g_ref, h_ref):
        xv = x_ref[...]
        rstd = lax.rsqrt(jnp.mean(xv * xv, axis=-1, keepdims=True) + NORM_EPS)
        h_ref[...] = (xv * rstd * g_ref[...]).astype(BF16)

    return pl.pallas_call(
        body, name="rmsnorm_fwd", grid=(S // tm,),
        in_specs=[pl.BlockSpec((tm, D), lambda i: (i, 0)), pl.BlockSpec((1, D), lambda i: (0, 0))],
        out_specs=pl.BlockSpec((tm, D), lambda i: (i, 0)),
        out_shape=jax.ShapeDtypeStruct((S, D), BF16), compiler_params=_params("arbitrary"))(x, g)


def ffn_up(h, wg, wu, l):
    S, D = h.shape
    F = wg.shape[-1]
    tm = _tile(S, 512)

    def body(h_ref, wg_ref, wu_ref, a_ref, b_ref, y_ref):
        hv = h_ref[...]
        a = _dot(hv, wg_ref[...])
        b = _dot(hv, wu_ref[...])
        a_ref[...] = a.astype(BF16)
        b_ref[...] = b.astype(BF16)
        y_ref[...] = (a * _sigmoid(a) * b).astype(BF16)

    out = pl.BlockSpec((None, tm, F), lambda j, i: (j, i, 0))
    shp = jax.ShapeDtypeStruct((N_CHIP, S, F), BF16)
    return pl.pallas_call(
        body, name="ffn_up", grid=(N_CHIP, S // tm),
        in_specs=[pl.BlockSpec((tm, D), lambda j, i: (i, 0)), _wspec(D, F, l, 0), _wspec(D, F, l, 0)],
        out_specs=[out, out, out], out_shape=[shp, shp, shp],
        compiler_params=_params("arbitrary", "arbitrary"))(h, wg, wu)


def residual_matmul(x, a, w, l, scale, a_blocked):
    S, D = x.shape
    Kb = w.shape[2]
    tm = _tile(S, 512)

    def body(x_ref, a_ref, w_ref, o_ref, acc):
        j = pl.program_id(1)

        @pl.when(j == 0)
        def _():
            acc[...] = jnp.zeros_like(acc)

        acc[...] += _dot(a_ref[...], w_ref[...])

        @pl.when(j == N_CHIP - 1)
        def _():
            o_ref[...] = x_ref[...] + scale * acc[...]

    a_spec = (pl.BlockSpec((None, tm, Kb), lambda i, j: (j, i, 0)) if a_blocked
              else pl.BlockSpec((tm, Kb), lambda i, j: (i, j)))
    return pl.pallas_call(
        body, name="residual_matmul", grid=(S // tm, N_CHIP),
        in_specs=[pl.BlockSpec((tm, D), lambda i, j: (i, 0)), a_spec, _wspec(Kb, D, l, 1)],
        out_specs=pl.BlockSpec((tm, D), lambda i, j: (i, 0)),
        out_shape=jax.ShapeDtypeStruct((S, D), F32),
        scratch_shapes=[pltpu.VMEM((tm, D), F32)],
        compiler_params=_params("arbitrary", "arbitrary"))(x, a, w)


def norm_bwd_matmul(x, g, dxo, pairs, l, g_blocked):
    S, D = x.shape
    Nb = pairs[0][1].shape[-1]
    tm = _tile(S, 256)
    n_p = len(pairs)

    def body(*refs):
        x_ref, g_ref, dxo_ref = refs[:3]
        g_refs = refs[3:3 + n_p]
        w_refs = refs[3 + n_p:3 + 2 * n_p]
        dx_ref, dg_ref, acc = refs[3 + 2 * n_p:]
        i, j = pl.program_id(0), pl.program_id(1)

        @pl.when(j == 0)
        def _():
            acc[...] = jnp.zeros_like(acc)

        @pl.when((i == 0) & (j == 0))
        def _():
            dg_ref[...] = jnp.zeros_like(dg_ref)

        for g_r, w_r in zip(g_refs, w_refs):
            acc[...] += _dot_nt(g_r[...], w_r[...])

        @pl.when(j == N_CHIP - 1)
        def _():
            xv = x_ref[...]
            rstd = lax.rsqrt(jnp.mean(xv * xv, axis=-1, keepdims=True) + NORM_EPS)
            xhat = xv * rstd
            dh = acc[...]
            gd = dh * g_ref[...]
            dx_ref[...] = dxo_ref[...] + rstd * (gd - xhat * jnp.mean(gd * xhat, axis=-1, keepdims=True))
            dg_ref[...] += jnp.sum(dh * xhat, axis=0, keepdims=True)

    row = pl.BlockSpec((tm, D), lambda i, j: (i, 0))
    vec = pl.BlockSpec((1, D), lambda i, j: (0, 0))
    g_spec = (pl.BlockSpec((None, tm, Nb), lambda i, j: (j, i, 0)) if g_blocked
              else pl.BlockSpec((tm, Nb), lambda i, j: (i, j)))
    return pl.pallas_call(
        body, name="norm_bwd_matmul", grid=(S // tm, N_CHIP),
        in_specs=[row, vec, row] + [g_spec] * n_p + [_wspec(D, Nb, l, 1)] * n_p,
        out_specs=[row, vec],
        out_shape=[jax.ShapeDtypeStruct((S, D), F32), jax.ShapeDtypeStruct((1, D), F32)],
        scratch_shapes=[pltpu.VMEM((tm, D), F32)],
        compiler_params=_params("arbitrary", "arbitrary"))(
            x, g, dxo, *[p[0] for p in pairs], *[p[1] for p in pairs])


def ffn_bwd_hidden(dxo, wd, a, b, l):
    S, D = dxo.shape
    F = wd.shape[2]
    tm = _tile(S, 512)

    def body(dxo_ref, wd_ref, a_ref, b_ref, da_ref, db_ref):
        dy = 0.5 * _dot_nt(dxo_ref[...].astype(BF16), wd_ref[...])
        av = a_ref[...].astype(F32)
        bv = b_ref[...].astype(F32)
        s = _sigmoid(av)
        da_ref[...] = (dy * bv * (s * (1.0 + av * (1.0 - s)))).astype(BF16)
        db_ref[...] = (dy * (av * s)).astype(BF16)

    blk = pl.BlockSpec((None, tm, F), lambda j, i: (j, i, 0))
    shp = jax.ShapeDtypeStruct((N_CHIP, S, F), BF16)
    return pl.pallas_call(
        body, name="ffn_bwd_hidden", grid=(N_CHIP, S // tm),
        in_specs=[pl.BlockSpec((tm, D), lambda j, i: (i, 0)), _wspec(F, D, l, 0), blk, blk],
        out_specs=[blk, blk], out_shape=[shp, shp],
        compiler_params=_params("arbitrary", "arbitrary"))(dxo, wd, a, b)


def weight_grad(a, b, m, n, a_blocked, b_blocked, b_scale=1.0):
    S = a.shape[-2]
    ts = _tile(S, 512)
    tm = m if m % 256 else _tile(m, 1024)
    tn = n if n % 256 else _tile(n, 1024)
    if m * n * 4 <= 6 * 2 ** 20:
        tm, tn = m, n

    def spec(arr, blocked, width, t, pos):
        if blocked:
            return pl.BlockSpec((None, ts, t), lambda j, mi, ni, si: (j, si, (mi, ni)[pos]))
        if arr.shape[-1] == width:
            return pl.BlockSpec((ts, t), lambda j, mi, ni, si: (si, (mi, ni)[pos]))
        nblk = width // t
        return pl.BlockSpec((ts, t), lambda j, mi, ni, si: (si, j * nblk + (mi, ni)[pos]))

    def body(a_ref, b_ref, o_ref):
        @pl.when(pl.program_id(3) == 0)
        def _():
            o_ref[...] = jnp.zeros_like(o_ref)

        bv = b_ref[...]
        if b_scale != 1.0:
            bv = b_scale * bv
        o_ref[...] += _dot_tn(a_ref[...], bv.astype(BF16))

    return pl.pallas_call(
        body, name="weight_grad", grid=(N_CHIP, m // tm, n // tn, S // ts),
        in_specs=[spec(a, a_blocked, m, tm, 0), spec(b, b_blocked, n, tn, 1)],
        out_specs=pl.BlockSpec((None, tm, tn), lambda j, mi, ni, si: (j, mi, ni)),
        out_shape=jax.ShapeDtypeStruct((N_CHIP, m, n), F32),
        compiler_params=_params("arbitrary", "arbitrary", "arbitrary", "arbitrary"))(a, b)


def mix_in(h, w_in, l):
    S, D = h.shape
    Nb = w_in.shape[-1]
    tm = _tile(S, 512)

    def body(h_ref, w_ref, z_ref, zb_ref):
        z = _dot(h_ref[...], w_ref[...])
        z_ref[...] = z
        zb_ref[...] = z.astype(BF16)

    out = pl.BlockSpec((tm, Nb), lambda j, i: (i, j))
    return pl.pallas_call(
        body, name="mix_in", grid=(N_CHIP, S // tm),
        in_specs=[pl.BlockSpec((tm, D), lambda j, i: (i, 0)), _wspec(D, Nb, l, 0)],
        out_specs=[out, out],
        out_shape=[jax.ShapeDtypeStruct((S, N_CHIP * Nb), F32), jax.ShapeDtypeStruct((S, N_CHIP * Nb), BF16)],
        compiler_params=_params("arbitrary", "arbitrary"))(h, w_in)


def mix_out_bwd(dx, w_out, l):
    S, D = dx.shape
    Kb = w_out.shape[2]
    tm = _tile(S, 512)

    def body(dx_ref, w_ref, dc_ref):
        dc_ref[...] = _dot_nt(dx_ref[...].astype(BF16), w_ref[...])

    return pl.pallas_call(
        body, name="mix_out_bwd", grid=(N_CHIP, S // tm),
        in_specs=[pl.BlockSpec((tm, D), lambda j, i: (i, 0)), _wspec(Kb, D, l, 0)],
        out_specs=pl.BlockSpec((tm, Kb), lambda j, i: (i, j)),
        out_shape=jax.ShapeDtypeStruct((S, N_CHIP * Kb), F32),
        compiler_params=_params("arbitrary", "arbitrary"))(dx, w_out)


def final_loss(x, g, target):
    S, D = x.shape
    tm = _tile(S, 256)

    def body(x_ref, g_ref, t_ref, loss_ref, dx_ref, dg_ref):
        @pl.when(pl.program_id(0) == 0)
        def _():
            loss_ref[...] = jnp.zeros_like(loss_ref)
            dg_ref[...] = jnp.zeros_like(dg_ref)

        xv = x_ref[...]
        rstd = lax.rsqrt(jnp.mean(xv * xv, axis=-1, keepdims=True) + NORM_EPS)
        xhat = xv * rstd
        err = xhat * g_ref[...] - t_ref[...]
        loss_ref[...] += jnp.sum(err * err, axis=0, keepdims=True)
        dy = err * (1.0 / D)
        gd = dy * g_ref[...]
        dx_ref[...] = rstd * (gd - xhat * jnp.mean(gd * xhat, axis=-1, keepdims=True))
        dg_ref[...] += jnp.sum(dy * xhat, axis=0, keepdims=True)

    row = pl.BlockSpec((tm, D), lambda i: (i, 0))
    vec = pl.BlockSpec((1, D), lambda i: (0, 0))
    return pl.pallas_call(
        body, name="final_loss", grid=(S // tm,), in_specs=[row, vec, row], out_specs=[vec, row, vec],
        out_shape=[jax.ShapeDtypeStruct((1, D), F32), jax.ShapeDtypeStruct((S, D), F32),
                   jax.ShapeDtypeStruct((1, D), F32)],
        compiler_params=_params("arbitrary"))(x, g, target)


def _tri(T, strict_lower):
    row = lax.broadcasted_iota(jnp.int32, (T, T), 0)
    col = lax.broadcasted_iota(jnp.int32, (T, T), 1)
    return jnp.where((row > col) if strict_lower else (row < col), 1.0, 0.0).astype(BF16)


def attn_fwd(zb, n_heads):
    S = zb.shape[0]
    T = _tile(S, 256)
    scale = HEAD ** -0.5

    def body(q_ref, k_ref, v_ref, o_ref, l_ref):
        i = pl.program_id(1)
        q = q_ref[...]
        later = _tri(T, True)
        causal = (lax.broadcasted_iota(jnp.int32, (T, T), 1) < lax.broadcasted_iota(jnp.int32, (T, T), 0))

        def tile(kb, carry, acc, diag):
            start = pl.multiple_of(kb * T, T)
            k = k_ref[pl.ds(start, T), :]
            v = v_ref[pl.ds(start, T), :]
            z = _dot_nt(q, k) * scale
            lf = _neg_softplus(z)
            if diag:
                lf = jnp.where(causal, lf, 0.0)
            hi, lo = _split_bf16(lf)
            after = _dot(hi, later) + _dot(lo, later)
            w = jnp.exp(z + lf + after + carry)
            if diag:
                w = jnp.where(causal, w, 0.0)
            acc = acc + _dot(w.astype(BF16), v)
            carry = carry + jnp.sum(lf, axis=1, keepdims=True)
            return carry, acc

        carry, acc = tile(i, jnp.zeros((T, 1), F32), jnp.zeros((T, HEAD), F32), True)
        carry, acc = lax.fori_loop(0, i, lambda n, ca: tile(i - 1 - n, ca[0], ca[1], False), (carry, acc))
        o_ref[...] = acc.astype(BF16)
        l_ref[...] = carry

    H = n_heads
    return pl.pallas_call(
        body, name="attn_fwd", grid=(H, S // T),
        in_specs=[pl.BlockSpec((T, HEAD), lambda h, i: (i, h)),
                  pl.BlockSpec((S, HEAD), lambda h, i: (0, H + h)),
                  pl.BlockSpec((S, HEAD), lambda h, i: (0, 2 * H + h))],
        out_specs=[pl.BlockSpec((T, HEAD), lambda h, i: (i, h)),
                   pl.BlockSpec((None, T, 1), lambda h, i: (h, i, 0))],
        out_shape=[jax.ShapeDtypeStruct((S, H * HEAD), BF16), jax.ShapeDtypeStruct((H, S, 1), F32)],
        compiler_params=_params("arbitrary", "arbitrary"))(zb, zb, zb)


def attn_bwd(zb, ltot, dc, n_heads):
    S = zb.shape[0]
    T = _tile(S, 256)
    scale = HEAD ** -0.5

    def body(q_ref, k_ref, v_ref, l_ref, do_ref, dq_ref, dk_ref, dv_ref):
        i = pl.program_id(1)

        @pl.when(i == 0)
        def _():
            dk_ref[...] = jnp.zeros_like(dk_ref)
            dv_ref[...] = jnp.zeros_like(dv_ref)

        q = q_ref[...]
        dob = do_ref[...].astype(BF16)
        ltot_row = l_ref[...]
        earlier = _tri(T, False)
        causal = (lax.broadcasted_iota(jnp.int32, (T, T), 1) < lax.broadcasted_iota(jnp.int32, (T, T), 0))

        def tile(kb, c_lf, c_g, dq, diag):
            start = pl.multiple_of(kb * T, T)
            k = k_ref[pl.ds(start, T), :]
            v = v_ref[pl.ds(start, T), :]
            z = _dot_nt(q, k) * scale
            lf = _neg_softplus(z)
            if diag:
                lf = jnp.where(causal, lf, 0.0)
            hi, lo = _split_bf16(lf)
            upto = _dot(hi, earlier) + _dot(lo, earlier) + lf + c_lf
            w = jnp.exp(z + lf + (ltot_row - upto))
            if diag:
                w = jnp.where(causal, w, 0.0)
            g = w * _dot_nt(dob, v)
            before = _dot(g.astype(BF16), earlier) + c_g
            sig = jnp.exp(z + lf)
            dz = g * (1.0 - sig) - before * sig
            if diag:
                dz = jnp.where(causal, dz, 0.0)
            dzb = dz.astype(BF16)
            dv_ref[pl.ds(start, T), :] += _dot_tn(w.astype(BF16), dob)
            dk_ref[pl.ds(start, T), :] += scale * _dot_tn(dzb, q)
            dq = dq + scale * _dot(dzb, k)
            return (c_lf + jnp.sum(lf, axis=1, keepdims=True), c_g + jnp.sum(g, axis=1, keepdims=True), dq)

        zero = jnp.zeros((T, 1), F32)
        c_lf, c_g, dq = lax.fori_loop(0, i, lambda kb, c: tile(kb, c[0], c[1], c[2], False),
                                      (zero, zero, jnp.zeros((T, HEAD), F32)))
        _, _, dq = tile(i, c_lf, c_g, dq, True)
        dq_ref[...] = dq

    H = n_heads
    full = pl.BlockSpec((S, HEAD), lambda h, i: (0, h))
    blk = pl.BlockSpec((T, HEAD), lambda h, i: (i, h))
    shp = jax.ShapeDtypeStruct((S, H * HEAD), F32)
    return pl.pallas_call(
        body, name="attn_bwd", grid=(H, S // T),
        in_specs=[blk, pl.BlockSpec((S, HEAD), lambda h, i: (0, H + h)),
                  pl.BlockSpec((S, HEAD), lambda h, i: (0, 2 * H + h)),
                  pl.BlockSpec((None, T, 1), lambda h, i: (h, i, 0)), blk],
        out_specs=[blk, full, full], out_shape=[shp, shp, shp],
        compiler_params=_params("arbitrary", "arbitrary"))(zb, zb, zb, ltot, dc)


def _rg_gates(u, ga, gx, lam):
    r = _sigmoid(ga)
    i = _sigmoid(gx)
    log_a = RG_C * r * _neg_softplus(-lam)
    a = jnp.exp(log_a)
    b = jnp.sqrt(-_expm1(2.0 * log_a)) * (i * u)
    return a, b


def _heads_matmul(u, w_ref, transpose):
    outs = []
    for hh in range(w_ref.shape[0]):
        uh = u[:, hh * HEAD:(hh + 1) * HEAD].astype(BF16)
        wv = w_ref[hh].astype(BF16)
        outs.append(_dot_nt(uh, wv) if transpose else _dot(uh, wv))
    return jnp.concatenate(outs, axis=1)


def _conv(buf, cw_ref, cb_ref, tb):
    u = cb_ref[...] + cw_ref[pl.ds(CONV_WIDTH - 1, 1), :] * buf[pl.ds(8, tb), :]
    for k in range(1, CONV_WIDTH):
        u = u + cw_ref[pl.ds(CONV_WIDTH - 1 - k, 1), :] * buf[pl.ds(8 - k, tb), :]
    return u


def _rnn_specs(tb, C, nb, col_gate, rev):
    blk_of = (lambda i: nb - 1 - i) if rev else (lambda i: i)
    blk = lambda col: pl.BlockSpec((tb, C), lambda i: (blk_of(i), col))
    halo = lambda col: pl.BlockSpec((8, C), lambda i: (jnp.maximum(blk_of(i) * (tb // 8) - 1, 0), col))
    return blk, halo


def rnn_fwd(z, cw, cb, wa, ba, wx, bx, lam):
    S = z.shape[0]
    C = cb.shape[-1]
    tb = _tile(S, 512)
    nb = S // tb

    def body(xg_ref, xr_ref, halo_ref, cw_ref, cb_ref, wa_ref, ba_ref, wx_ref, bx_ref, lam_ref,
             out_ref, hs_ref, buf, a_s, b_s, h_c):
        i = pl.program_id(0)

        @pl.when(i == 0)
        def _():
            h_c[...] = jnp.zeros_like(h_c)

        buf[pl.ds(0, 8), :] = jnp.where(i == 0, 0.0, halo_ref[...])
        buf[pl.ds(8, tb), :] = xr_ref[...]
        u = _conv(buf, cw_ref, cb_ref, tb)
        ga = _heads_matmul(u, wa_ref, False) + ba_ref[...]
        gx = _heads_matmul(u, wx_ref, False) + bx_ref[...]
        a, b = _rg_gates(u, ga, gx, lam_ref[...])
        a_s[...] = a
        b_s[...] = b

        def step(t, h):
            h = a_s[pl.ds(t, 1), :] * h + b_s[pl.ds(t, 1), :]
            hs_ref[pl.ds(t, 1), :] = h
            return h

        h_c[...] = lax.fori_loop(0, tb, step, h_c[...], unroll=8)
        out_ref[...] = (_gelu(xg_ref[...]) * hs_ref[...]).astype(BF16)

    blk, halo = _rnn_specs(tb, C, nb, 6, False)
    full = lambda a: pl.BlockSpec(a.shape, lambda i: (0,) * a.ndim)
    out = pl.BlockSpec((tb, C), lambda i: (i, 0))
    return pl.pallas_call(
        body, name="rnn_fwd", grid=(nb,),
        in_specs=[blk(6), blk(7), halo(7)] + [full(a) for a in (cw, cb, wa, ba, wx, bx, lam)],
        out_specs=[out, out],
        out_shape=[jax.ShapeDtypeStruct((S, C), BF16), jax.ShapeDtypeStruct((S, C), F32)],
        scratch_shapes=[pltpu.VMEM((tb + 8, C), F32), pltpu.VMEM((tb, C), F32), pltpu.VMEM((tb, C), F32),
                        pltpu.VMEM((1, C), F32)],
        compiler_params=_params("arbitrary"))(z, z, z, cw, cb, wa, ba, wx, bx, lam)


def rnn_bwd(z, hs, dc, cw, cb, wa, ba, wx, bx, lam):
    S = z.shape[0]
    C = cb.shape[-1]
    tb = _tile(S, 512)
    nb = S // tb

    def body(xg_ref, xr_ref, halo_ref, hs_ref, hs_halo_ref, do_ref,
             cw_ref, cb_ref, wa_ref, ba_ref, wx_ref, bx_ref, lam_ref,
             dxg_ref, dxr_ref, dcw_ref, dcb_ref, dwa_ref, dba_ref, dwx_ref, dbx_ref, dlam_ref,
             buf, hbuf, dubuf, a_s, dh_s, carry):
        i = pl.program_id(0)
        first = i == nb - 1

        @pl.when(i == 0)
        def _():
            carry[...] = jnp.zeros_like(carry)
            dubuf[pl.ds(tb, 8), :] = jnp.zeros((8, C), F32)
            for r in (dcw_ref, dcb_ref, dwa_ref, dba_ref, dwx_ref, dbx_ref, dlam_ref):
                r[...] = jnp.zeros_like(r)

        buf[pl.ds(0, 8), :] = jnp.where(first, 0.0, halo_ref[...])
        buf[pl.ds(8, tb), :] = xr_ref[...]
        u = _conv(buf, cw_ref, cb_ref, tb)
        ga = _heads_matmul(u, wa_ref, False) + ba_ref[...]
        gx = _heads_matmul(u, wx_ref, False) + bx_ref[...]
        (a, _), gates_vjp = jax.vjp(_rg_gates, u, ga, gx, lam_ref[...])
        gel, gelu_vjp = jax.vjp(_gelu, xg_ref[...])
        hs = hs_ref[...]
        do = do_ref[...]
        dxg_ref[...] = gelu_vjp(do * hs)[0].astype(BF16)
        dh_s[...] = do * gel
        a_s[...] = a

        def step(n, c):
            t = tb - 1 - n
            acc = dh_s[pl.ds(t, 1), :] + c
            dh_s[pl.ds(t, 1), :] = acc
            return a_s[pl.ds(t, 1), :] * acc

        carry[...] = lax.fori_loop(0, tb, step, carry[...], unroll=8)
        db = dh_s[...]
        hbuf[pl.ds(0, 8), :] = jnp.where(first, 0.0, hs_halo_ref[...])
        hbuf[pl.ds(8, tb), :] = hs
        da = db * hbuf[pl.ds(7, tb), :]
        du, dga, dgx, dlam = gates_vjp((da, db))
        du = du + _heads_matmul(dga, wa_ref, True) + _heads_matmul(dgx, wx_ref, True)
        dlam_ref[...] += dlam
        dba_ref[...] += jnp.sum(dga, axis=0, keepdims=True)
        dbx_ref[...] += jnp.sum(dgx, axis=0, keepdims=True)
        ub = u.astype(BF16)
        for hh in range(wa_ref.shape[0]):
            cols = slice(hh * HEAD, (hh + 1) * HEAD)
            dwa_ref[hh] += _dot_tn(ub[:, cols], dga[:, cols].astype(BF16))
            dwx_ref[hh] += _dot_tn(ub[:, cols], dgx[:, cols].astype(BF16))
        dcb_ref[...] += jnp.sum(du, axis=0, keepdims=True)
        dubuf[pl.ds(0, tb), :] = du
        dxr = cw_ref[pl.ds(CONV_WIDTH - 1, 1), :] * du
        dcw_ref[pl.ds(CONV_WIDTH - 1, 1), :] += jnp.sum(du * buf[pl.ds(8, tb), :], axis=0, keepdims=True)
        for k in range(1, CONV_WIDTH):
            dxr = dxr + cw_ref[pl.ds(CONV_WIDTH - 1 - k, 1), :] * dubuf[pl.ds(k, tb), :]
            dcw_ref[pl.ds(CONV_WIDTH - 1 - k, 1), :] += jnp.sum(du * buf[pl.ds(8 - k, tb), :], axis=0,
                                                                keepdims=True)
        dxr_ref[...] = dxr.astype(BF16)
        dubuf[pl.ds(tb, 8), :] = du[0:8, :]

    blk, halo = _rnn_specs(tb, C, nb, 6, True)
    full = lambda a: pl.BlockSpec(a.shape, lambda i: (0,) * a.ndim)
    out = pl.BlockSpec((tb, C), lambda i: (nb - 1 - i, 0))
    params = (cw, cb, wa, ba, wx, bx, lam)
    return pl.pallas_call(
        body, name="rnn_bwd", grid=(nb,),
        in_specs=[blk(6), blk(7), halo(7), blk(0), halo(0), blk(2)] + [full(a) for a in params],
        out_specs=[out, out] + [full(a) for a in params],
        out_shape=[jax.ShapeDtypeStruct((S, C), BF16)] * 2 + [jax.ShapeDtypeStruct(a.shape, F32) for a in params],
        scratch_shapes=[pltpu.VMEM((tb + 8, C), F32), pltpu.VMEM((tb + 8, C), F32), pltpu.VMEM((tb + 8, C), F32),
                        pltpu.VMEM((tb, C), F32), pltpu.VMEM((tb, C), F32), pltpu.VMEM((1, C), F32)],
        compiler_params=_params("arbitrary"))(z, z, z, hs, hs, dc, *params)


def _pool_stats(buf, x, t0, tb, n_groups):
    t = t0 + lax.broadcasted_iota(jnp.int32, (tb, 1), 0)
    ds, cnts = [], []
    for g in range(n_groups):
        win = POOL_WINDOWS[g]
        cols = slice(g * HEAD, (g + 1) * HEAD)
        s = buf[pl.ds(16, tb), cols]
        for k in range(1, win):
            s = s + buf[pl.ds(16 - k, tb), cols]
        cnt = jnp.minimum(t + 1, win).astype(F32)
        ds.append(s / cnt - x[:, cols])
        cnts.append(cnt)
    return ds, cnts


def _pool_specs(tb, C, nb, col, rev):
    blk_of = (lambda i: nb - 1 - i) if rev else (lambda i: i)
    blk = pl.BlockSpec((tb, C), lambda i: (blk_of(i), col))
    halo = pl.BlockSpec((16, C), lambda i: (jnp.maximum(blk_of(i) * (tb // 16) - 1, 0), col))
    return blk, halo


def pool_fwd(z, pw, ps):
    S = z.shape[0]
    C = ps.shape[-1]
    G = pw.shape[0]
    tb = _tile(S, 512)
    nb = S // tb

    def body(x_ref, halo_ref, pw_ref, ps_ref, out_ref, buf):
        i = pl.program_id(0)
        buf[pl.ds(0, 16), :] = jnp.where(i == 0, 0.0, halo_ref[...])
        x = x_ref[...]
        buf[pl.ds(16, tb), :] = x
        ds, _ = _pool_stats(buf, x, i * tb, tb, G)
        y = jnp.concatenate([_dot(ds[g].astype(BF16), pw_ref[g].astype(BF16)) for g in range(G)], axis=1)
        out_ref[...] = (y * ps_ref[...]).astype(BF16)

    blk, halo = _pool_specs(tb, C, nb, 8, False)
    full = lambda a: pl.BlockSpec(a.shape, lambda i: (0,) * a.ndim)
    return pl.pallas_call(
        body, name="pool_fwd", grid=(nb,), in_specs=[blk, halo, full(pw), full(ps)],
        out_specs=pl.BlockSpec((tb, C), lambda i: (i, 0)), out_shape=jax.ShapeDtypeStruct((S, C), BF16),
        scratch_shapes=[pltpu.VMEM((tb + 16, C), F32)], compiler_params=_params("arbitrary"))(z, z, pw, ps)


def pool_bwd(z, dc, pw, ps):
    S = z.shape[0]
    C = ps.shape[-1]
    G = pw.shape[0]
    tb = _tile(S, 512)
    nb = S // tb

    def body(x_ref, halo_ref, dy_ref, pw_ref, ps_ref, dx_ref, dpw_ref, dps_ref, buf, ebuf):
        i = pl.program_id(0)
        ib = nb - 1 - i

        @pl.when(i == 0)
        def _():
            ebuf[pl.ds(tb, 16), :] = jnp.zeros((16, C), F32)
            dpw_ref[...] = jnp.zeros_like(dpw_ref)
            dps_ref[...] = jnp.zeros_like(dps_ref)

        buf[pl.ds(0, 16), :] = jnp.where(ib == 0, 0.0, halo_ref[...])
        x = x_ref[...]
        buf[pl.ds(16, tb), :] = x
        ds, cnts = _pool_stats(buf, x, ib * tb, tb, G)
        dy = dy_ref[...]
        dyp = dy * ps_ref[...]
        ypre, dds, es = [], [], []
        for g in range(G):
            cols = slice(g * HEAD, (g + 1) * HEAD)
            db = ds[g].astype(BF16)
            wv = pw_ref[g].astype(BF16)
            dypb = dyp[:, cols].astype(BF16)
            ypre.append(_dot(db, wv))
            dd = _dot_nt(dypb, wv)
            dpw_ref[g] += _dot_tn(db, dypb)
            dds.append(dd)
            es.append(dd / cnts[g])
        dps_ref[...] += jnp.sum(dy * jnp.concatenate(ypre, axis=1), axis=0, keepdims=True)
        e = jnp.concatenate(es, axis=1)
        ebuf[pl.ds(0, tb), :] = e
        dxs = []
        for g in range(G):
            cols = slice(g * HEAD, (g + 1) * HEAD)
            s = es[g]
            for k in range(1, POOL_WINDOWS[g]):
                s = s + ebuf[pl.ds(k, tb), cols]
            dxs.append(s - dds[g])
        dx_ref[...] = jnp.concatenate(dxs, axis=1).astype(BF16)
        ebuf[pl.ds(tb, 16), :] = e[0:16, :]

    blk, halo = _pool_specs(tb, C, nb, 8, True)
    dyb, _ = _pool_specs(tb, C, nb, 3, True)
    full = lambda a: pl.BlockSpec(a.shape, lambda i: (0,) * a.ndim)
    return pl.pallas_call(
        body, name="pool_bwd", grid=(nb,), in_specs=[blk, halo, dyb, full(pw), full(ps)],
        out_specs=[pl.BlockSpec((tb, C), lambda i: (nb - 1 - i, 0)), full(pw), full(ps)],
        out_shape=[jax.ShapeDtypeStruct((S, C), BF16), jax.ShapeDtypeStruct(pw.shape, F32),
                   jax.ShapeDtypeStruct(ps.shape, F32)],
        scratch_shapes=[pltpu.VMEM((tb + 16, C), F32), pltpu.VMEM((tb + 16, C), F32)],
        compiler_params=_params("arbitrary"))(z, z, dc, pw, ps)


def adamw(w, g, m, v):
    R, N = w.shape
    tr = _tile(R, 256)

    def body(w_ref, g_ref, m_ref, v_ref, d_ref, m2_ref, v2_ref):
        gv = g_ref[...]
        m2 = ADAM_B1 * m_ref[...] + (1.0 - ADAM_B1) * gv
        v2 = ADAM_B2 * v_ref[...] + (1.0 - ADAM_B2) * (gv * gv)
        m_hat = m2 / (1.0 - ADAM_B1 ** ADAM_STEP)
        v_hat = v2 / (1.0 - ADAM_B2 ** ADAM_STEP)
        d_ref[...] = -ADAM_LR * (m_hat / (jnp.sqrt(v_hat) + ADAM_EPS) + ADAM_WD * w_ref[...])
        m2_ref[...] = m2
        v2_ref[...] = v2

    blk = pl.BlockSpec((tr, N), lambda i: (i, 0))
    shp = jax.ShapeDtypeStruct((R, N), F32)
    return pl.pallas_call(body, name="adamw", grid=(R // tr,), in_specs=[blk] * 4, out_specs=[blk] * 3,
                          out_shape=[shp] * 3, compiler_params=_params("arbitrary"))(w, g, m, v)


HBM = pl.BlockSpec(memory_space=pltpu.HBM)


def _place():
    x, y, c = lax.axis_index("x"), lax.axis_index("y"), lax.axis_index("c")
    others = [(1 - x, y), (x, 1 - y), (1 - x, 1 - y)]
    return x, y, c, others


def gather_weight(w):
    def body(w_ref, out_ref, send_sems, recv_sems, local_sem):
        x, y, c, others = _place()
        me = 2 * x + y
        sibling = (x, y, 1 - c)

        def copy(k, chip, layer, to, src=None):
            dst = out_ref.at[2 * chip[0] + chip[1], layer]
            return pltpu.make_async_remote_copy(
                src_ref=dst if src is None else src, dst_ref=dst, send_sem=send_sems.at[k],
                recv_sem=recv_sems.at[k], device_id=to, device_id_type=MESH)

        mine = pltpu.make_async_copy(w_ref, out_ref.at[me], local_sem)
        mine.start()
        first = [copy(k, (x, y), c, (*chip, c), src=w_ref.at[c]) for k, chip in enumerate(others)]
        for cp in first:
            cp.start()
        passed = [copy(3 + k, chip, c, sibling) for k, chip in enumerate(others)]
        for k, chip in enumerate(others):
            copy(k, chip, c, (x, y, c)).wait_recv()
            passed[k].start()
        for k, chip in enumerate(others):
            copy(3 + k, chip, 1 - c, (x, y, c)).wait_recv()
        for cp in first + passed:
            cp.wait_send()
        mine.wait()

    return pl.pallas_call(
        body, name="gather_weight", in_specs=[HBM], out_specs=HBM,
        out_shape=jax.ShapeDtypeStruct((N_CHIP,) + w.shape, w.dtype),
        scratch_shapes=[pltpu.SemaphoreType.DMA((6,)), pltpu.SemaphoreType.DMA((6,)), pltpu.SemaphoreType.DMA],
    )(w)


def swap_half_rows(g):
    K = g.shape[1]
    Kh = K // 2

    def body(g_ref, out_ref, send_sem, recv_sem):
        x, y, c, _ = _place()
        cp = pltpu.make_async_remote_copy(
            src_ref=g_ref.at[:, pl.ds((1 - c) * Kh, Kh), :], dst_ref=out_ref, send_sem=send_sem,
            recv_sem=recv_sem, device_id=(x, y, 1 - c), device_id_type=MESH)
        cp.start()
        cp.wait()

    return pl.pallas_call(
        body, name="swap_half_rows", in_specs=[HBM], out_specs=HBM,
        out_shape=jax.ShapeDtypeStruct((N_CHIP, Kh, g.shape[2]), g.dtype),
        scratch_shapes=[pltpu.SemaphoreType.DMA, pltpu.SemaphoreType.DMA])(g)


def chip_sum(g, got, c_arr):
    _, K, N = g.shape
    Kh = K // 2
    tr = _tile(Kh, 352 if Kh % 352 == 0 else 256)

    def body(c_ref, g_ref, got_ref, o_ref):
        o_ref[...] = (g_ref[...] + got_ref[...]).astype(BF16)

    nt = Kh // tr
    return pl.pallas_call(
        body, name="chip_sum",
        grid_spec=pltpu.PrefetchScalarGridSpec(
            num_scalar_prefetch=1, grid=(N_CHIP, nt),
            in_specs=[pl.BlockSpec((None, tr, N), lambda j, i, c: (j, c[0] * nt + i, 0)),
                      pl.BlockSpec((None, tr, N), lambda j, i, c: (j, i, 0))],
            out_specs=pl.BlockSpec((None, tr, N), lambda j, i, c: (j, i, 0))),
        out_shape=jax.ShapeDtypeStruct((N_CHIP, Kh, N), BF16),
        compiler_params=_params("arbitrary", "arbitrary"))(c_arr, g, got)


def scatter_chip_sums(cs):
    def body(cs_ref, out_ref, send_sems, recv_sems):
        x, y, c, others = _place()
        cps = [pltpu.make_async_remote_copy(
            src_ref=cs_ref.at[2 * chip[0] + chip[1]], dst_ref=out_ref.at[k], send_sem=send_sems.at[k],
            recv_sem=recv_sems.at[k], device_id=(*chip, c), device_id_type=MESH)
            for k, chip in enumerate(others)]
        for cp in cps:
            cp.start()
        for cp in cps:
            cp.wait()

    return pl.pallas_call(
        body, name="scatter_chip_sums", in_specs=[HBM], out_specs=HBM,
        out_shape=jax.ShapeDtypeStruct((3,) + cs.shape[1:], cs.dtype),
        scratch_shapes=[pltpu.SemaphoreType.DMA((3,)), pltpu.SemaphoreType.DMA((3,))])(cs)


def total_sum(cs, got, me_arr):
    _, Kh, N = cs.shape
    tr = _tile(Kh, 352 if Kh % 352 == 0 else 256)

    def body(me_ref, cs_ref, got_ref, o_ref):
        o_ref[...] = ((cs_ref[...].astype(F32) + got_ref[0].astype(F32)) + got_ref[1].astype(F32)) + \
            got_ref[2].astype(F32)

    return pl.pallas_call(
        body, name="total_sum",
        grid_spec=pltpu.PrefetchScalarGridSpec(
            num_scalar_prefetch=1, grid=(Kh // tr,),
            in_specs=[pl.BlockSpec((None, tr, N), lambda i, me: (me[0], i, 0)),
                      pl.BlockSpec((3, tr, N), lambda i, me: (0, i, 0))],
            out_specs=pl.BlockSpec((tr, N), lambda i, me: (i, 0))),
        out_shape=jax.ShapeDtypeStruct((Kh, N), F32),
        compiler_params=_params("arbitrary"))(me_arr, cs, got)


def join_halves(t0, t1):
    Kh, N = t0.shape

    def body(t0_ref, t1_ref, out_ref, send_sems, recv_sems, local_sems):
        x, y, c, _ = _place()
        cps, locs = [], []
        for l, t_ref in enumerate((t0_ref, t1_ref)):
            dst = out_ref.at[l, pl.ds(c * Kh, Kh), :]
            locs.append(pltpu.make_async_copy(t_ref, dst, local_sems.at[l]))
            cps.append(pltpu.make_async_remote_copy(
                src_ref=t_ref, dst_ref=dst, send_sem=send_sems.at[l], recv_sem=recv_sems.at[l],
                device_id=(x, y, 1 - c), device_id_type=MESH))
        for cp in locs + cps:
            cp.start()
        for l in range(2):
            cps[l].wait()
            locs[l].wait()

    return pl.pallas_call(
        body, name="join_halves", in_specs=[HBM, HBM], out_specs=HBM,
        out_shape=jax.ShapeDtypeStruct((2, 2 * Kh, N), F32),
        scratch_shapes=[pltpu.SemaphoreType.DMA((2,)), pltpu.SemaphoreType.DMA((2,)),
                        pltpu.SemaphoreType.DMA((2,))])(t0, t1)


def allreduce_small(p):
    R = p.shape[0]

    def body(p_ref, out_ref, all_ref, send_sems, recv_sems, local_sem):
        x, y, c, others = _place()
        me, sibling = (x, y, c), (x, y, 1 - c)

        def rows(px, py, pc):
            return all_ref.at[pl.ds((4 * px + 2 * py + pc) * R, R), :]

        def copy(k, block, to, src=None):
            return pltpu.make_async_remote_copy(
                src_ref=rows(*block) if src is None else src, dst_ref=rows(*block), send_sem=send_sems.at[k],
                recv_sem=recv_sems.at[k], device_id=to, device_id_type=MESH)

        mine = pltpu.make_async_copy(p_ref, rows(*me), local_sem)
        mine.start()
        first = [copy(0, me, sibling, src=p_ref)]
        first += [copy(1 + j, me, (*chip, c), src=p_ref) for j, chip in enumerate(others)]
        for cp in first:
            cp.start()
        passed = [copy(4 + j, (*chip, c), sibling) for j, chip in enumerate(others)]
        for j, chip in enumerate(others):
            copy(1 + j, (*chip, c), me).wait_recv()
            passed[j].start()
        copy(0, sibling, me).wait_recv()
        for j, chip in enumerate(others):
            copy(4 + j, (*chip, 1 - c), me).wait_recv()
        for cp in first + passed:
            cp.wait_send()
        mine.wait()
        total = all_ref[pl.ds(0, R), :]
        for d in range(1, N_DEV):
            total = total + all_ref[pl.ds(d * R, R), :]
        out_ref[...] = total

    vmem = pl.BlockSpec(memory_space=pltpu.VMEM)
    return pl.pallas_call(
        body, name="allreduce_small", in_specs=[vmem], out_specs=vmem,
        out_shape=jax.ShapeDtypeStruct((R, 128), F32),
        scratch_shapes=[pltpu.VMEM((N_DEV * R, 128), F32), pltpu.SemaphoreType.DMA((7,)),
                        pltpu.SemaphoreType.DMA((7,)), pltpu.SemaphoreType.DMA],
        compiler_params=pltpu.CompilerParams(vmem_limit_bytes=VMEM_LIMIT_V7X))(p)


def _ffn_fwd(x, g, wg, wu, wd, l):
    h = rmsnorm_fwd(x, g)
    a, b, y = ffn_up(h, wg, wu, l)
    return residual_matmul(x, y, wd, l, 0.5, True), (x, h, a, b, y)


def _ffn_bwd(dxo, saved, g, wg, wu, wd, l):
    x, h, a, b, y = saved
    D = x.shape[1]
    F = wg.shape[-1]
    da, db = ffn_bwd_hidden(dxo, wd, a, b, l)
    g_wd = weight_grad(y, dxo, F, D, True, False, b_scale=0.5)
    g_wg = weight_grad(h, da, D, F, False, True)
    g_wu = weight_grad(h, db, D, F, False, True)
    dx, dg = norm_bwd_matmul(x, g, dxo, [(da, wg), (db, wu)], l, True)
    return dx, dg, g_wg, g_wu, g_wd


def kernel(x, norm_ffn1, ffn1_gate, ffn1_up, ffn1_down, norm_mix, w_in, conv_w, conv_b, rg_w_a, rg_b_a, rg_w_x, rg_b_x, rg_lambda, pool_w, pool_scale, w_out, norm_ffn2, ffn2_gate, ffn2_up, ffn2_down, norm_final, loss_target, m_norm_ffn1, m_ffn1_gate, m_ffn1_up, m_ffn1_down, m_norm_mix, m_w_in, m_conv_w, m_conv_b, m_rg_w_a, m_rg_b_a, m_rg_w_x, m_rg_b_x, m_rg_lambda, m_pool_w, m_pool_scale, m_w_out, m_norm_ffn2, m_ffn2_gate, m_ffn2_up, m_ffn2_down, m_norm_final, v_norm_ffn1, v_ffn1_gate, v_ffn1_up, v_ffn1_down, v_norm_mix, v_w_in, v_conv_w, v_conv_b, v_rg_w_a, v_rg_b_a, v_rg_w_x, v_rg_b_x, v_rg_lambda, v_pool_w, v_pool_scale, v_w_out, v_norm_ffn2, v_ffn2_gate, v_ffn2_up, v_ffn2_down, v_norm_final):
    given = dict(locals())
    W = {n: given[n] for n in WEIGHTS}
    M = {n: given["m_" + n] for n in WEIGHTS}
    V = {n: given["v_" + n] for n in WEIGHTS}
    depth = norm_ffn1.shape[0]
    S, D = x.shape[1], x.shape[2]
    xs = x.reshape(S, D)
    target = loss_target.reshape(S, D)
    C = conv_b.shape[-1]
    n_heads = (D // 2) // HEAD
    cx, cy, cc = lax.axis_index("x"), lax.axis_index("y"), lax.axis_index("c")
    me = 2 * cx + cy
    me_arr = jnp.reshape(me, (1,)).astype(jnp.int32)
    c_arr = jnp.reshape(cc, (1,)).astype(jnp.int32)

    Wg = {n: gather_weight(W[n].astype(BF16)) for n in BIG}
    conv_all = gather_weight(conv_w)
    conv_full = jnp.transpose(conv_all, (1, 2, 0, 3)).reshape(depth, CONV_WIDTH, C)

    row = lambda a, l: a[l].reshape(1, -1)

    saved = []
    cur = xs
    for l in range(depth):
        cur, s1 = _ffn_fwd(cur, row(norm_ffn1, l), Wg['ffn1_gate'], Wg['ffn1_up'], Wg['ffn1_down'], l)
        x1 = cur
        hm = rmsnorm_fwd(x1, row(norm_mix, l))
        z, zb = mix_in(hm, Wg['w_in'], l)
        att, ltot = attn_fwd(zb, n_heads)
        rnn_p = (conv_full[l], row(conv_b, l), rg_w_a[l], row(rg_b_a, l), rg_w_x[l], row(rg_b_x, l),
                 row(rg_lambda, l))
        rnn, hs = rnn_fwd(z, *rnn_p)
        pool = pool_fwd(z, pool_w[l], row(pool_scale, l))
        cat = jnp.concatenate([att, rnn, pool], axis=1)
        cur = residual_matmul(x1, cat, Wg['w_out'], l, 1.0, False)
        x2 = cur
        cur, s2 = _ffn_fwd(cur, row(norm_ffn2, l), Wg['ffn2_gate'], Wg['ffn2_up'], Wg['ffn2_down'], l)
        saved.append((s1, (x1, hm, z, zb, ltot, hs, cat, rnn_p), s2))

    loss_cols, dx, g_norm_final = final_loss(cur, norm_final.reshape(1, D), target)

    G = {n: [None] * depth for n in WEIGHTS if n != 'norm_final'}
    for l in reversed(range(depth)):
        s1, (x1, hm, z, zb, ltot, hs, cat, rnn_p), s2 = saved[l]
        dx, dg, g_wg, g_wu, g_wd = _ffn_bwd(dx, s2, row(norm_ffn2, l), Wg['ffn2_gate'], Wg['ffn2_up'],
                                            Wg['ffn2_down'], l)
        G['norm_ffn2'][l], G['ffn2_gate'][l], G['ffn2_up'][l], G['ffn2_down'][l] = dg, g_wg, g_wu, g_wd
        dc = mix_out_bwd(dx, Wg['w_out'], l)
        G['w_out'][l] = weight_grad(cat, dx, C, D, False, False)
        dq, dk, dv = attn_bwd(zb, ltot, dc, n_heads)
        (dxg, dxr, G['conv_w'][l], G['conv_b'][l], G['rg_w_a'][l], G['rg_b_a'][l], G['rg_w_x'][l],
         G['rg_b_x'][l], G['rg_lambda'][l]) = rnn_bwd(z, hs, dc, *rnn_p)
        dxp, G['pool_w'][l], G['pool_scale'][l] = pool_bwd(z, dc, pool_w[l], row(pool_scale, l))
        dz = jnp.concatenate([dq.astype(BF16), dk.astype(BF16), dv.astype(BF16), dxg, dxr, dxp], axis=1)
        Nb = dz.shape[1] // N_CHIP
        G['w_in'][l] = weight_grad(hm, dz, D, Nb, False, False)
        dx, G['norm_mix'][l] = norm_bwd_matmul(x1, row(norm_mix, l), dx, [(dz, Wg['w_in'])], l, False)
        dx, dg, g_wg, g_wu, g_wd = _ffn_bwd(dx, s1, row(norm_ffn1, l), Wg['ffn1_gate'], Wg['ffn1_up'],
                                            Wg['ffn1_down'], l)
        G['norm_ffn1'][l], G['ffn1_gate'][l], G['ffn1_up'][l], G['ffn1_down'][l] = dg, g_wg, g_wu, g_wd
    grad_x = dx.reshape(x.shape)

    outs = {}
    for n in BIG:
        halves = []
        for l in range(depth):
            g = G[n][l]
            cs = chip_sum(g, swap_half_rows(g), c_arr)
            halves.append(total_sum(cs, scatter_chip_sums(cs), me_arr))
        g_red = join_halves(*halves)
        two_d = (-1, g_red.shape[-1])
        d, m2, v2 = adamw(W[n].reshape(two_d), g_red.reshape(two_d), M[n].reshape(two_d), V[n].reshape(two_d))
        outs[n] = (g_red, d.reshape(W[n].shape), m2.reshape(W[n].shape), v2.reshape(W[n].shape))

    loss_part = 0.5 * jnp.sum(loss_cols) / D
    parts = [jnp.stack(G[n]).reshape(-1) for n in SMALL if n != 'norm_final']
    parts += [g_norm_final.reshape(-1), jnp.reshape(loss_part, (1,))]
    sizes = [p.shape[0] for p in parts]
    total = sum(sizes)
    R = -(-total // 1024) * 8
    packed = jnp.concatenate(parts + [jnp.zeros((R * 128 - total,), F32)]).reshape(R, 128)
    red = allreduce_small(packed).reshape(-1)
    offs = [sum(sizes[:i]) for i in range(len(sizes))]
    small_names = [n for n in SMALL if n != 'norm_final'] + ['norm_final']
    small_g = {}
    for n, o, sz in zip(small_names, offs, sizes):
        if n == 'conv_w':
            full = red[o:o + sz].reshape(depth, CONV_WIDTH, C)
            small_g[n] = lax.dynamic_slice_in_dim(full, me * (C // N_CHIP), C // N_CHIP, axis=2)
        else:
            small_g[n] = red[o:o + sz].reshape(W[n].shape)
    loss = red[offs[-1]]

    def pack(d):
        flat = jnp.concatenate([d[n].reshape(-1) for n in small_names])
        rows = -(-flat.shape[0] // 1024) * 8
        return jnp.concatenate([flat, jnp.ones((rows * 128 - flat.shape[0],), F32)]).reshape(rows, 128)

    d_s, m_s, v_s = adamw(pack(W), pack(small_g), pack(M), pack(V))
    o = 0
    for n in small_names:
        sz = W[n].size
        outs[n] = (small_g[n],) + tuple(a.reshape(-1)[o:o + sz].reshape(W[n].shape) for a in (d_s, m_s, v_s))
        o += sz

    return (loss, grad_x, *[outs[n][0] for n in WEIGHTS], *[outs[n][1] for n in WEIGHTS],
            *[outs[n][2] for n in WEIGHTS], *[outs[n][3] for n in WEIGHTS])
```

```python
import functools

import jax
import jax.numpy as jnp
from jax import lax
from jax.experimental import pallas as pl
from jax.experimental.pallas import tpu as pltpu

F32 = jnp.float32
BF16 = jnp.bfloat16
MESH = pl.DeviceIdType.MESH

HEAD = 128
ATTN_TQ, ATTN_TK = 1024, 256
ATTN_TQ_FWD = 2048
POOL_WINDOWS = (2, 4, 8, 16)
CONV_WIDTH = 4
RG_C = 8.0
NORM_EPS = 1e-6
N_CHIP = 4
N_DEV = 8
ADAM_LR, ADAM_B1, ADAM_B2, ADAM_EPS, ADAM_WD, ADAM_STEP = 0.001, 0.9, 0.999, 1e-08, 0.01, 10
VMEM_LIMIT_V7X = 56 * 2 ** 20

WEIGHTS = ['norm_ffn1', 'ffn1_gate', 'ffn1_up', 'ffn1_down', 'norm_mix', 'w_in', 'conv_w', 'conv_b', 'rg_w_a',
           'rg_b_a', 'rg_w_x', 'rg_b_x', 'rg_lambda', 'pool_w', 'pool_scale', 'w_out', 'norm_ffn2', 'ffn2_gate',
           'ffn2_up', 'ffn2_down', 'norm_final']
BIG = ['ffn1_gate', 'ffn1_up', 'ffn1_down', 'w_in', 'w_out', 'ffn2_gate', 'ffn2_up', 'ffn2_down']
SMALL = [n for n in WEIGHTS if n not in BIG]


def _params(*sem):
    return pltpu.CompilerParams(dimension_semantics=sem, vmem_limit_bytes=VMEM_LIMIT_V7X)


def _tile(n, pref):
    t = min(n, pref)
    while n % t:
        t //= 2
    return t


def _dot(a, b):
    return jnp.dot(a, b, preferred_element_type=F32)


def _dot_nt(a, b):
    return lax.dot_general(a, b, (((1,), (1,)), ((), ())), preferred_element_type=F32)


def _dot_tn(a, b):
    return lax.dot_general(a, b, (((0,), (0,)), ((), ())), preferred_element_type=F32)


def _sigmoid(x):
    return 1.0 / (1.0 + jnp.exp(-x))


def _gelu(x):
    return 0.5 * x * (1.0 + jnp.tanh(0.7978845608028654 * (x + 0.044715 * x * x * x)))


def _expm1(x):
    series = x * (1.0 + x * 0.5 * (1.0 + x * (1.0 / 3.0) * (1.0 + x * 0.25 * (1.0 + x * 0.2))))
    return jnp.where(jnp.abs(x) < 0.1, series, jnp.exp(x) - 1.0)


def _neg_softplus(z):
    return -(jnp.maximum(z, 0.0) + jnp.log(1.0 + jnp.exp(-jnp.abs(z))))


def _split_bf16(x):
    hi = x.astype(BF16)
    return hi, (x - hi.astype(F32)).astype(BF16)


def _wspec(k, n, l, jpos):
    return pl.BlockSpec((None, None, k, n), lambda *g: (g[jpos], l, 0, 0))


def rmsnorm_fwd(x, g):
    S, D = x.shape
    tm = _tile(S, 512)

    def body(x_ref, g_ref, h_ref):
        xv = x_ref[...]
        rstd = lax.rsqrt(jnp.mean(xv * xv, axis=-1, keepdims=True) + NORM_EPS)
        h_ref[...] = (xv * rstd * g_ref[...]).astype(BF16)

    return pl.pallas_call(
        body, name="rmsnorm_fwd", grid=(S // tm,),
        in_specs=[pl.BlockSpec((tm, D), lambda i: (i, 0)), pl.BlockSpec((1, D), lambda i: (0, 0))],
        out_specs=pl.BlockSpec((tm, D), lambda i: (i, 0)),
        out_shape=jax.ShapeDtypeStruct((S, D), BF16), compiler_params=_params("arbitrary"))(x, g)


def ffn_up(h, wg, wu, l):
    S, D = h.shape
    F = wg.shape[-1]
    tm = _tile(S, 512)

    def body(h_ref, wg_ref, wu_ref, a_ref, b_ref, y_ref):
        hv = h_ref[...]
        a = _dot(hv, wg_ref[...])
        b = _dot(hv, wu_ref[...])
        a_ref[...] = a.astype(BF16)
        b_ref[...] = b.astype(BF16)
        y_ref[...] = (a * _sigmoid(a) * b).astype(BF16)

    out = pl.BlockSpec((None, tm, F), lambda j, i: (j, i, 0))
    shp = jax.ShapeDtypeStruct((N_CHIP, S, F), BF16)
    return pl.pallas_call(
        body, name="ffn_up", grid=(N_CHIP, S // tm),
        in_specs=[pl.BlockSpec((tm, D), lambda j, i: (i, 0)), _wspec(D, F, l, 0), _wspec(D, F, l, 0)],
        out_specs=[out, out, out], out_shape=[shp, shp, shp],
        compiler_params=_params("arbitrary", "arbitrary"))(h, wg, wu)


def residual_matmul(x, a, w, l, scale, a_blocked):
    S, D = x.shape
    Kb = w.shape[2]
    tm = _tile(S, 512)

    def body(x_ref, a_ref, w_ref, o_ref, acc):
        j = pl.program_id(1)

        @pl.when(j == 0)
        def _():
            acc[...] = jnp.zeros_like(acc)

        acc[...] += _dot(a_ref[...], w_ref[...])

        @pl.when(j == N_CHIP - 1)
        def _():
            o_ref[...] = x_ref[...] + scale * acc[...]

    a_spec = (pl.BlockSpec((None, tm, Kb), lambda i, j: (j, i, 0)) if a_blocked
              else pl.BlockSpec((tm, Kb), lambda i, j: (i, j)))
    return pl.pallas_call(
        body, name="residual_matmul", grid=(S // tm, N_CHIP),
        in_specs=[pl.BlockSpec((tm, D), lambda i, j: (i, 0)), a_spec, _wspec(Kb, D, l, 1)],
        out_specs=pl.BlockSpec((tm, D), lambda i, j: (i, 0)),
        out_shape=jax.ShapeDtypeStruct((S, D), F32),
        scratch_shapes=[pltpu.VMEM((tm, D), F32)],
        compiler_params=_params("arbitrary", "arbitrary"))(x, a, w)


def norm_bwd_matmul(x, g, dxo, pairs, l, g_blocked):
    S, D = x.shape
    Nb = pairs[0][1].shape[-1]
    tm = _tile(S, 256)
    n_p = len(pairs)

    def body(*refs):
        x_ref, g_ref, dxo_ref = refs[:3]
        g_refs = refs[3:3 + n_p]
        w_refs = refs[3 + n_p:3 + 2 * n_p]
        dx_ref, dg_ref, acc = refs[3 + 2 * n_p:]
        i, j = pl.program_id(0), pl.program_id(1)

        @pl.when(j == 0)
        def _():
            acc[...] = jnp.zeros_like(acc)

        @pl.when((i == 0) & (j == 0))
        def _():
            dg_ref[...] = jnp.zeros_like(dg_ref)

        for g_r, w_r in zip(g_refs, w_refs):
            acc[...] += _dot_nt(g_r[...], w_r[...])

        @pl.when(j == N_CHIP - 1)
        def _():
            xv = x_ref[...]
            rstd = lax.rsqrt(jnp.mean(xv * xv, axis=-1, keepdims=True) + NORM_EPS)
            xhat = xv * rstd
            dh = acc[...]
            gd = dh * g_ref[...]
            dx_ref[...] = dxo_ref[...] + rstd * (gd - xhat * jnp.mean(gd * xhat, axis=-1, keepdims=True))
            dg_ref[...] += jnp.sum(dh * xhat, axis=0, keepdims=True)

    row = pl.BlockSpec((tm, D), lambda i, j: (i, 0))
    vec = pl.BlockSpec((1, D), lambda i, j: (0, 0))
    g_spec = (pl.BlockSpec((None, tm, Nb), lambda i, j: (j, i, 0)) if g_blocked
              else pl.BlockSpec((tm, Nb), lambda i, j: (i, j)))
    return pl.pallas_call(
        body, name="norm_bwd_matmul", grid=(S // tm, N_CHIP),
        in_specs=[row, vec, row] + [g_spec] * n_p + [_wspec(D, Nb, l, 1)] * n_p,
        out_specs=[row, vec],
        out_shape=[jax.ShapeDtypeStruct((S, D), F32), jax.ShapeDtypeStruct((1, D), F32)],
        scratch_shapes=[pltpu.VMEM((tm, D), F32)],
        compiler_params=_params("arbitrary", "arbitrary"))(
            x, g, dxo, *[p[0] for p in pairs], *[p[1] for p in pairs])


def ffn_bwd_hidden(dxo, wd, a, b, l):
    S, D = dxo.shape
    F = wd.shape[2]
    tm = _tile(S, 512)

    def body(dxo_ref, wd_ref, a_ref, b_ref, da_ref, db_ref):
        dy = 0.5 * _dot_nt(dxo_ref[...].astype(BF16), wd_ref[...])
        av = a_ref[...].astype(F32)
        bv = b_ref[...].astype(F32)
        s = _sigmoid(av)
        da_ref[...] = (dy * bv * (s * (1.0 + av * (1.0 - s)))).astype(BF16)
        db_ref[...] = (dy * (av * s)).astype(BF16)

    blk = pl.BlockSpec((None, tm, F), lambda j, i: (j, i, 0))
    shp = jax.ShapeDtypeStruct((N_CHIP, S, F), BF16)
    return pl.pallas_call(
        body, name="ffn_bwd_hidden", grid=(N_CHIP, S // tm),
        in_specs=[pl.BlockSpec((tm, D), lambda j, i: (i, 0)), _wspec(F, D, l, 0), blk, blk],
        out_specs=[blk, blk], out_shape=[shp, shp],
        compiler_params=_params("arbitrary", "arbitrary"))(dxo, wd, a, b)


def weight_grad(a, b, m, n, a_blocked, b_blocked, b_scale=1.0):
    S = a.shape[-2]
    ts = _tile(S, 512)
    tm = m if m % 256 else _tile(m, 1024)
    tn = n if n % 256 else _tile(n, 1024)
    if m * n * 4 <= 6 * 2 ** 20:
        tm, tn = m, n

    def spec(arr, blocked, width, t, pos):
        if blocked:
            return pl.BlockSpec((None, ts, t), lambda j, mi, ni, si: (j, si, (mi, ni)[pos]))
        if arr.shape[-1] == width:
            return pl.BlockSpec((ts, t), lambda j, mi, ni, si: (si, (mi, ni)[pos]))
        nblk = width // t
        return pl.BlockSpec((ts, t), lambda j, mi, ni, si: (si, j * nblk + (mi, ni)[pos]))

    def body(a_ref, b_ref, o_ref):
        @pl.when(pl.program_id(3) == 0)
        def _():
            o_ref[...] = jnp.zeros_like(o_ref)

        bv = b_ref[...]
        if b_scale != 1.0:
            bv = b_scale * bv
        o_ref[...] += _dot_tn(a_ref[...], bv.astype(BF16))

    return pl.pallas_call(
        body, name="weight_grad", grid=(N_CHIP, m // tm, n // tn, S // ts),
        in_specs=[spec(a, a_blocked, m, tm, 0), spec(b, b_blocked, n, tn, 1)],
        out_specs=pl.BlockSpec((None, tm, tn), lambda j, mi, ni, si: (j, mi, ni)),
        out_shape=jax.ShapeDtypeStruct((N_CHIP, m, n), F32),
        compiler_params=_params("arbitrary", "arbitrary", "arbitrary", "arbitrary"))(a, b)


def mix_in(h, w_in, l):
    S, D = h.shape
    Nb = w_in.shape[-1]
    tm = _tile(S, 512)

    def body(h_ref, w_ref, z_ref, zb_ref):
        z = _dot(h_ref[...], w_ref[...])
        z_ref[...] = z
        zb_ref[...] = z.astype(BF16)

    out = pl.BlockSpec((tm, Nb), lambda j, i: (i, j))
    return pl.pallas_call(
        body, name="mix_in", grid=(N_CHIP, S // tm),
        in_specs=[pl.BlockSpec((tm, D), lambda j, i: (i, 0)), _wspec(D, Nb, l, 0)],
        out_specs=[out, out],
        out_shape=[jax.ShapeDtypeStruct((S, N_CHIP * Nb), F32), jax.ShapeDtypeStruct((S, N_CHIP * Nb), BF16)],
        compiler_params=_params("arbitrary", "arbitrary"))(h, w_in)


def mix_out_bwd(dx, w_out, l):
    S, D = dx.shape
    Kb = w_out.shape[2]
    tm = _tile(S, 512)

    def body(dx_ref, w_ref, dc_ref):
        dc_ref[...] = _dot_nt(dx_ref[...].astype(BF16), w_ref[...])

    return pl.pallas_call(
        body, name="mix_out_bwd", grid=(N_CHIP, S // tm),
        in_specs=[pl.BlockSpec((tm, D), lambda j, i: (i, 0)), _wspec(Kb, D, l, 0)],
        out_specs=pl.BlockSpec((tm, Kb), lambda j, i: (i, j)),
        out_shape=jax.ShapeDtypeStruct((S, N_CHIP * Kb), F32),
        compiler_params=_params("arbitrary", "arbitrary"))(dx, w_out)


def final_loss(x, g, target):
    S, D = x.shape
    tm = _tile(S, 256)

    def body(x_ref, g_ref, t_ref, loss_ref, dx_ref, dg_ref):
        @pl.when(pl.program_id(0) == 0)
        def _():
            loss_ref[...] = jnp.zeros_like(loss_ref)
            dg_ref[...] = jnp.zeros_like(dg_ref)

        xv = x_ref[...]
        rstd = lax.rsqrt(jnp.mean(xv * xv, axis=-1, keepdims=True) + NORM_EPS)
        xhat = xv * rstd
        err = xhat * g_ref[...] - t_ref[...]
        loss_ref[...] += jnp.sum(err * err, axis=0, keepdims=True)
        dy = err * (1.0 / D)
        gd = dy * g_ref[...]
        dx_ref[...] = rstd * (gd - xhat * jnp.mean(gd * xhat, axis=-1, keepdims=True))
        dg_ref[...] += jnp.sum(dy * xhat, axis=0, keepdims=True)

    row = pl.BlockSpec((tm, D), lambda i: (i, 0))
    vec = pl.BlockSpec((1, D), lambda i: (0, 0))
    return pl.pallas_call(
        body, name="final_loss", grid=(S // tm,), in_specs=[row, vec, row], out_specs=[vec, row, vec],
        out_shape=[jax.ShapeDtypeStruct((1, D), F32), jax.ShapeDtypeStruct((S, D), F32),
                   jax.ShapeDtypeStruct((1, D), F32)],
        compiler_params=_params("arbitrary"))(x, g, target)


def _tri(T, strict_lower):
    row = lax.broadcasted_iota(jnp.int32, (T, T), 0)
    col = lax.broadcasted_iota(jnp.int32, (T, T), 1)
    return jnp.where((row > col) if strict_lower else (row < col), 1.0, 0.0).astype(BF16)


def _causal(rows, cols):
    return lax.broadcasted_iota(jnp.int32, (rows, cols), 1) < lax.broadcasted_iota(jnp.int32, (rows, cols), 0)


def attn_fwd(zb, n_heads):
    S = zb.shape[0]
    TQ = _tile(S, ATTN_TQ_FWD)
    TK = min(ATTN_TK, TQ)
    R = TQ // TK
    scale = HEAD ** -0.5

    def body(q_ref, k_ref, v_ref, o_ref, l_ref):
        i = pl.program_id(1)
        q = q_ref[...]
        later = _tri(TK, True)

        def tile(qr, k, v, carry, acc, mask):
            z = _dot_nt(qr, k) * scale
            lf = _neg_softplus(z)
            if mask is not None:
                lf = jnp.where(mask, lf, 0.0)
            hi, lo = _split_bf16(lf)
            after = _dot(hi, later) + _dot(lo, later)
            w = jnp.exp(z + lf + after + carry)
            if mask is not None:
                w = jnp.where(mask, w, 0.0)
            acc = acc + _dot(w.astype(BF16), v)
            carry = carry + jnp.sum(lf, axis=1, keepdims=True)
            return carry, acc

        def kv(kb):
            start = pl.multiple_of(kb * TK, TK)
            return k_ref[pl.ds(start, TK), :], v_ref[pl.ds(start, TK), :]

        carry, acc = jnp.zeros((TQ, 1), F32), jnp.zeros((TQ, HEAD), F32)
        for r in reversed(range(R)):
            k, v = kv(i * R + r)
            c_r, a_r = tile(q[r * TK:], k, v, carry[r * TK:], acc[r * TK:], _causal(TQ - r * TK, TK))
            carry = c_r if r == 0 else jnp.concatenate([carry[:r * TK], c_r], axis=0)
            acc = a_r if r == 0 else jnp.concatenate([acc[:r * TK], a_r], axis=0)
        carry, acc = lax.fori_loop(0, i * R, lambda n, ca: tile(q, *kv(i * R - 1 - n), ca[0], ca[1], None),
                                   (carry, acc))
        o_ref[...] = acc.astype(BF16)
        l_ref[...] = carry

    H = n_heads
    return pl.pallas_call(
        body, name="attn_fwd", grid=(H, S // TQ),
        in_specs=[pl.BlockSpec((TQ, HEAD), lambda h, i: (i, h)),
                  pl.BlockSpec((S, HEAD), lambda h, i: (0, H + h)),
                  pl.BlockSpec((S, HEAD), lambda h, i: (0, 2 * H + h))],
        out_specs=[pl.BlockSpec((TQ, HEAD), lambda h, i: (i, h)),
                   pl.BlockSpec((None, TQ, 1), lambda h, i: (h, i, 0))],
        out_shape=[jax.ShapeDtypeStruct((S, H * HEAD), BF16), jax.ShapeDtypeStruct((H, S, 1), F32)],
        compiler_params=_params("arbitrary", "arbitrary"))(zb, zb, zb)


def attn_bwd(zb, ltot, dc, n_heads):
    S = zb.shape[0]
    TQ = _tile(S, ATTN_TQ)
    TK = min(ATTN_TK, TQ)
    R = TQ // TK
    scale = HEAD ** -0.5

    def body(q_ref, k_ref, v_ref, l_ref, do_ref, dq_ref, dk_ref, dv_ref):
        i = pl.program_id(1)

        @pl.when(i == 0)
        def _():
            dk_ref[...] = jnp.zeros_like(dk_ref)
            dv_ref[...] = jnp.zeros_like(dv_ref)

        q = q_ref[...]
        dob = do_ref[...].astype(BF16)
        ltot_rows = l_ref[...]
        earlier = _tri(TK, False)

        def tile(kb, r0, c_lf, c_g, dq, mask):
            start = pl.multiple_of(kb * TK, TK)
            k = k_ref[pl.ds(start, TK), :]
            v = v_ref[pl.ds(start, TK), :]
            z = _dot_nt(q[r0:], k) * scale
            lf = _neg_softplus(z)
            if mask is not None:
                lf = jnp.where(mask, lf, 0.0)
            hi, lo = _split_bf16(lf)
            upto = _dot(hi, earlier) + _dot(lo, earlier) + lf + c_lf
            w = jnp.exp(z + lf + (ltot_rows[r0:] - upto))
            if mask is not None:
                w = jnp.where(mask, w, 0.0)
            g = w * _dot_nt(dob[r0:], v)
            before = _dot(g.astype(BF16), earlier) + c_g
            sig = jnp.exp(z + lf)
            dz = g * (1.0 - sig) - before * sig
            if mask is not None:
                dz = jnp.where(mask, dz, 0.0)
            dzb = dz.astype(BF16)
            dv_ref[pl.ds(start, TK), :] += _dot_tn(w.astype(BF16), dob[r0:])
            dk_ref[pl.ds(start, TK), :] += scale * _dot_tn(dzb, q[r0:])
            dq = dq + scale * _dot(dzb, k)
            return (c_lf + jnp.sum(lf, axis=1, keepdims=True), c_g + jnp.sum(g, axis=1, keepdims=True), dq)

        zero = jnp.zeros((TQ, 1), F32)
        c_lf, c_g, dq = lax.fori_loop(0, i * R, lambda kb, c: tile(kb, 0, c[0], c[1], c[2], None),
                                      (zero, zero, jnp.zeros((TQ, HEAD), F32)))
        for r in range(R):
            r0 = r * TK
            c_r, g_r, dq_r = tile(i * R + r, r0, c_lf[r0:], c_g[r0:], dq[r0:], _causal(TQ - r0, TK))
            if r == 0:
                c_lf, c_g, dq = c_r, g_r, dq_r
            else:
                c_lf = jnp.concatenate([c_lf[:r0], c_r], axis=0)
                c_g = jnp.concatenate([c_g[:r0], g_r], axis=0)
                dq = jnp.concatenate([dq[:r0], dq_r], axis=0)
        dq_ref[...] = dq

    H = n_heads
    full = pl.BlockSpec((S, HEAD), lambda h, i: (0, h))
    blk = pl.BlockSpec((TQ, HEAD), lambda h, i: (i, h))
    shp = jax.ShapeDtypeStruct((S, H * HEAD), F32)
    return pl.pallas_call(
        body, name="attn_bwd", grid=(H, S // TQ),
        in_specs=[blk, pl.BlockSpec((S, HEAD), lambda h, i: (0, H + h)),
                  pl.BlockSpec((S, HEAD), lambda h, i: (0, 2 * H + h)),
                  pl.BlockSpec((None, TQ, 1), lambda h, i: (h, i, 0)), blk],
        out_specs=[blk, full, full], out_shape=[shp, shp, shp],
        compiler_params=_params("arbitrary", "arbitrary"))(zb, zb, zb, ltot, dc)


def _rg_gates(u, ga, gx, lam):
    r = _sigmoid(ga)
    i = _sigmoid(gx)
    log_a = RG_C * r * _neg_softplus(-lam)
    a = jnp.exp(log_a)
    b = jnp.sqrt(-_expm1(2.0 * log_a)) * (i * u)
    return a, b


def _heads_matmul(u, w_ref, transpose):
    outs = []
    for hh in range(w_ref.shape[0]):
        uh = u[:, hh * HEAD:(hh + 1) * HEAD].astype(BF16)
        wv = w_ref[hh].astype(BF16)
        outs.append(_dot_nt(uh, wv) if transpose else _dot(uh, wv))
    return jnp.concatenate(outs, axis=1)


def _conv(buf, cw_ref, cb_ref, tb):
    u = cb_ref[...] + cw_ref[pl.ds(CONV_WIDTH - 1, 1), :] * buf[pl.ds(8, tb), :]
    for k in range(1, CONV_WIDTH):
        u = u + cw_ref[pl.ds(CONV_WIDTH - 1 - k, 1), :] * buf[pl.ds(8 - k, tb), :]
    return u


def _rnn_specs(tb, C, nb, col_gate, rev):
    blk_of = (lambda i: nb - 1 - i) if rev else (lambda i: i)
    blk = lambda col: pl.BlockSpec((tb, C), lambda i: (blk_of(i), col))
    halo = lambda col: pl.BlockSpec((8, C), lambda i: (jnp.maximum(blk_of(i) * (tb // 8) - 1, 0), col))
    return blk, halo


def rnn_fwd(z, cw, cb, wa, ba, wx, bx, lam):
    S = z.shape[0]
    C = cb.shape[-1]
    tb = _tile(S, 512)
    nb = S // tb

    def body(xg_ref, xr_ref, halo_ref, cw_ref, cb_ref, wa_ref, ba_ref, wx_ref, bx_ref, lam_ref,
             out_ref, hs_ref, buf, a_s, b_s, h_c):
        i = pl.program_id(0)

        @pl.when(i == 0)
        def _():
            h_c[...] = jnp.zeros_like(h_c)

        buf[pl.ds(0, 8), :] = jnp.where(i == 0, 0.0, halo_ref[...])
        buf[pl.ds(8, tb), :] = xr_ref[...]
        u = _conv(buf, cw_ref, cb_ref, tb)
        ga = _heads_matmul(u, wa_ref, False) + ba_ref[...]
        gx = _heads_matmul(u, wx_ref, False) + bx_ref[...]
        a, b = _rg_gates(u, ga, gx, lam_ref[...])
        a_s[...] = a
        b_s[...] = b

        def step(t, h):
            h = a_s[pl.ds(t, 1), :] * h + b_s[pl.ds(t, 1), :]
            hs_ref[pl.ds(t, 1), :] = h
            return h

        h_c[...] = lax.fori_loop(0, tb, step, h_c[...], unroll=8)
        out_ref[...] = (_gelu(xg_ref[...]) * hs_ref[...]).astype(BF16)

    blk, halo = _rnn_specs(tb, C, nb, 6, False)
    full = lambda a: pl.BlockSpec(a.shape, lambda i: (0,) * a.ndim)
    out = pl.BlockSpec((tb, C), lambda i: (i, 0))
    return pl.pallas_call(
        body, name="rnn_fwd", grid=(nb,),
        in_specs=[blk(6), blk(7), halo(7)] + [full(a) for a in (cw, cb, wa, ba, wx, bx, lam)],
        out_specs=[out, out],
        out_shape=[jax.ShapeDtypeStruct((S, C), BF16), jax.ShapeDtypeStruct((S, C), F32)],
        scratch_shapes=[pltpu.VMEM((tb + 8, C), F32), pltpu.VMEM((tb, C), F32), pltpu.VMEM((tb, C), F32),
                        pltpu.VMEM((1, C), F32)],
        compiler_params=_params("arbitrary"))(z, z, z, cw, cb, wa, ba, wx, bx, lam)


def rnn_bwd(z, hs, dc, cw, cb, wa, ba, wx, bx, lam):
    S = z.shape[0]
    C = cb.shape[-1]
    tb = _tile(S, 512)
    nb = S // tb

    def body(xg_ref, xr_ref, halo_ref, hs_ref, hs_halo_ref, do_ref,
             cw_ref, cb_ref, wa_ref, ba_ref, wx_ref, bx_ref, lam_ref,
             dxg_ref, dxr_ref, dcw_ref, dcb_ref, dwa_ref, dba_ref, dwx_ref, dbx_ref, dlam_ref,
             buf, hbuf, dubuf, a_s, dh_s, carry):
        i = pl.program_id(0)
        first = i == nb - 1

        @pl.when(i == 0)
        def _():
            carry[...] = jnp.zeros_like(carry)
            dubuf[pl.ds(tb, 8), :] = jnp.zeros((8, C), F32)
            for r in (dcw_ref, dcb_ref, dwa_ref, dba_ref, dwx_ref, dbx_ref, dlam_ref):
                r[...] = jnp.zeros_like(r)

        buf[pl.ds(0, 8), :] = jnp.where(first, 0.0, halo_ref[...])
        buf[pl.ds(8, tb), :] = xr_ref[...]
        u = _conv(buf, cw_ref, cb_ref, tb)
        ga = _heads_matmul(u, wa_ref, False) + ba_ref[...]
        gx = _heads_matmul(u, wx_ref, False) + bx_ref[...]
        (a, _), gates_vjp = jax.vjp(_rg_gates, u, ga, gx, lam_ref[...])
        gel, gelu_vjp = jax.vjp(_gelu, xg_ref[...])
        hs = hs_ref[...]
        do = do_ref[...]
        dxg_ref[...] = gelu_vjp(do * hs)[0].astype(BF16)
        dh_s[...] = do * gel
        a_s[...] = a

        def step(n, c):
            t = tb - 1 - n
            acc = dh_s[pl.ds(t, 1), :] + c
            dh_s[pl.ds(t, 1), :] = acc
            return a_s[pl.ds(t, 1), :] * acc

        carry[...] = lax.fori_loop(0, tb, step, carry[...], unroll=8)
        db = dh_s[...]
        hbuf[pl.ds(0, 8), :] = jnp.where(first, 0.0, hs_halo_ref[...])
        hbuf[pl.ds(8, tb), :] = hs
        da = db * hbuf[pl.ds(7, tb), :]
        du, dga, dgx, dlam = gates_vjp((da, db))
        du = du + _heads_matmul(dga, wa_ref, True) + _heads_matmul(dgx, wx_ref, True)
        dlam_ref[...] += dlam
        dba_ref[...] += jnp.sum(dga, axis=0, keepdims=True)
        dbx_ref[...] += jnp.sum(dgx, axis=0, keepdims=True)
        ub = u.astype(BF16)
        for hh in range(wa_ref.shape[0]):
            cols = slice(hh * HEAD, (hh + 1) * HEAD)
            dwa_ref[hh] += _dot_tn(ub[:, cols], dga[:, cols].astype(BF16))
            dwx_ref[hh] += _dot_tn(ub[:, cols], dgx[:, cols].astype(BF16))
        dcb_ref[...] += jnp.sum(du, axis=0, keepdims=True)
        dubuf[pl.ds(0, tb), :] = du
        dxr = cw_ref[pl.ds(CONV_WIDTH - 1, 1), :] * du
        dcw_ref[pl.ds(CONV_WIDTH - 1, 1), :] += jnp.sum(du * buf[pl.ds(8, tb), :], axis=0, keepdims=True)
        for k in range(1, CONV_WIDTH):
            dxr = dxr + cw_ref[pl.ds(CONV_WIDTH - 1 - k, 1), :] * dubuf[pl.ds(k, tb), :]
            dcw_ref[pl.ds(CONV_WIDTH - 1 - k, 1), :] += jnp.sum(du * buf[pl.ds(8 - k, tb), :], axis=0,
                                                                keepdims=True)
        dxr_ref[...] = dxr.astype(BF16)
        dubuf[pl.ds(tb, 8), :] = du[0:8, :]

    blk, halo = _rnn_specs(tb, C, nb, 6, True)
    full = lambda a: pl.BlockSpec(a.shape, lambda i: (0,) * a.ndim)
    out = pl.BlockSpec((tb, C), lambda i: (nb - 1 - i, 0))
    params = (cw, cb, wa, ba, wx, bx, lam)
    return pl.pallas_call(
        body, name="rnn_bwd", grid=(nb,),
        in_specs=[blk(6), blk(7), halo(7), blk(0), halo(0), blk(2)] + [full(a) for a in params],
        out_specs=[out, out] + [full(a) for a in params],
        out_shape=[jax.ShapeDtypeStruct((S, C), BF16)] * 2 + [jax.ShapeDtypeStruct(a.shape, F32) for a in params],
        scratch_shapes=[pltpu.VMEM((tb + 8, C), F32), pltpu.VMEM((tb + 8, C), F32), pltpu.VMEM((tb + 8, C), F32),
                        pltpu.VMEM((tb, C), F32), pltpu.VMEM((tb, C), F32), pltpu.VMEM((1, C), F32)],
        compiler_params=_params("arbitrary"))(z, z, z, hs, hs, dc, *params)


def _pool_stats(buf, x, t0, tb, n_groups):
    t = t0 + lax.broadcasted_iota(jnp.int32, (tb, 1), 0)
    ds, cnts = [], []
    for g in range(n_groups):
        win = POOL_WINDOWS[g]
        cols = slice(g * HEAD, (g + 1) * HEAD)
        s = buf[pl.ds(16, tb), cols]
        for k in range(1, win):
            s = s + buf[pl.ds(16 - k, tb), cols]
        cnt = jnp.minimum(t + 1, win).astype(F32)
        ds.append(s / cnt - x[:, cols])
        cnts.append(cnt)
    return ds, cnts


def _pool_specs(tb, C, nb, col, rev):
    blk_of = (lambda i: nb - 1 - i) if rev else (lambda i: i)
    blk = pl.BlockSpec((tb, C), lambda i: (blk_of(i), col))
    halo = pl.BlockSpec((16, C), lambda i: (jnp.maximum(blk_of(i) * (tb // 16) - 1, 0), col))
    return blk, halo


def pool_fwd(z, pw, ps):
    S = z.shape[0]
    C = ps.shape[-1]
    G = pw.shape[0]
    tb = _tile(S, 512)
    nb = S // tb

    def body(x_ref, halo_ref, pw_ref, ps_ref, out_ref, buf):
        i = pl.program_id(0)
        buf[pl.ds(0, 16), :] = jnp.where(i == 0, 0.0, halo_ref[...])
        x = x_ref[...]
        buf[pl.ds(16, tb), :] = x
        ds, _ = _pool_stats(buf, x, i * tb, tb, G)
        y = jnp.concatenate([_dot(ds[g].astype(BF16), pw_ref[g].astype(BF16)) for g in range(G)], axis=1)
        out_ref[...] = (y * ps_ref[...]).astype(BF16)

    blk, halo = _pool_specs(tb, C, nb, 8, False)
    full = lambda a: pl.BlockSpec(a.shape, lambda i: (0,) * a.ndim)
    return pl.pallas_call(
        body, name="pool_fwd", grid=(nb,), in_specs=[blk, halo, full(pw), full(ps)],
        out_specs=pl.BlockSpec((tb, C), lambda i: (i, 0)), out_shape=jax.ShapeDtypeStruct((S, C), BF16),
        scratch_shapes=[pltpu.VMEM((tb + 16, C), F32)], compiler_params=_params("arbitrary"))(z, z, pw, ps)


def pool_bwd(z, dc, pw, ps):
    S = z.shape[0]
    C = ps.shape[-1]
    G = pw.shape[0]
    tb = _tile(S, 512)
    nb = S // tb

    def body(x_ref, halo_ref, dy_ref, pw_ref, ps_ref, dx_ref, dpw_ref, dps_ref, buf, ebuf):
        i = pl.program_id(0)
        ib = nb - 1 - i

        @pl.when(i == 0)
        def _():
            ebuf[pl.ds(tb, 16), :] = jnp.zeros((16, C), F32)
            dpw_ref[...] = jnp.zeros_like(dpw_ref)
            dps_ref[...] = jnp.zeros_like(dps_ref)

        buf[pl.ds(0, 16), :] = jnp.where(ib == 0, 0.0, halo_ref[...])
        x = x_ref[...]
        buf[pl.ds(16, tb), :] = x
        ds, cnts = _pool_stats(buf, x, ib * tb, tb, G)
        dy = dy_ref[...]
        dyp = dy * ps_ref[...]
        ypre, dds, es = [], [], []
        for g in range(G):
            cols = slice(g * HEAD, (g + 1) * HEAD)
            db = ds[g].astype(BF16)
            wv = pw_ref[g].astype(BF16)
            dypb = dyp[:, cols].astype(BF16)
            ypre.append(_dot(db, wv))
            dd = _dot_nt(dypb, wv)
            dpw_ref[g] += _dot_tn(db, dypb)
            dds.append(dd)
            es.append(dd / cnts[g])
        dps_ref[...] += jnp.sum(dy * jnp.concatenate(ypre, axis=1), axis=0, keepdims=True)
        e = jnp.concatenate(es, axis=1)
        ebuf[pl.ds(0, tb), :] = e
        dxs = []
        for g in range(G):
            cols = slice(g * HEAD, (g + 1) * HEAD)
            s = es[g]
            for k in range(1, POOL_WINDOWS[g]):
                s = s + ebuf[pl.ds(k, tb), cols]
            dxs.append(s - dds[g])
        dx_ref[...] = jnp.concatenate(dxs, axis=1).astype(BF16)
        ebuf[pl.ds(tb, 16), :] = e[0:16, :]

    blk, halo = _pool_specs(tb, C, nb, 8, True)
    dyb, _ = _pool_specs(tb, C, nb, 3, True)
    full = lambda a: pl.BlockSpec(a.shape, lambda i: (0,) * a.ndim)
    return pl.pallas_call(
        body, name="pool_bwd", grid=(nb,), in_specs=[blk, halo, dyb, full(pw), full(ps)],
        out_specs=[pl.BlockSpec((tb, C), lambda i: (nb - 1 - i, 0)), full(pw), full(ps)],
        out_shape=[jax.ShapeDtypeStruct((S, C), BF16), jax.ShapeDtypeStruct(pw.shape, F32),
                   jax.ShapeDtypeStruct(ps.shape, F32)],
        scratch_shapes=[pltpu.VMEM((tb + 16, C), F32), pltpu.VMEM((tb + 16, C), F32)],
        compiler_params=_params("arbitrary"))(z, z, dc, pw, ps)


def adamw(w, g, m, v):
    R, N = w.shape
    tr = _tile(R, 256)

    def body(w_ref, g_ref, m_ref, v_ref, d_ref, m2_ref, v2_ref):
        gv = g_ref[...]
        m2 = ADAM_B1 * m_ref[...] + (1.0 - ADAM_B1) * gv
        v2 = ADAM_B2 * v_ref[...] + (1.0 - ADAM_B2) * (gv * gv)
        m_hat = m2 / (1.0 - ADAM_B1 ** ADAM_STEP)
        v_hat = v2 / (1.0 - ADAM_B2 ** ADAM_STEP)
        d_ref[...] = -ADAM_LR * (m_hat / (jnp.sqrt(v_hat) + ADAM_EPS) + ADAM_WD * w_ref[...])
        m2_ref[...] = m2
        v2_ref[...] = v2

    blk = pl.BlockSpec((tr, N), lambda i: (i, 0))
    shp = jax.ShapeDtypeStruct((R, N), F32)
    return pl.pallas_call(body, name="adamw", grid=(R // tr,), in_specs=[blk] * 4, out_specs=[blk] * 3,
                          out_shape=[shp] * 3, compiler_params=_params("arbitrary"))(w, g, m, v)


HBM = pl.BlockSpec(memory_space=pltpu.HBM)


def _place():
    x, y, c = lax.axis_index("x"), lax.axis_index("y"), lax.axis_index("c")
    others = [(1 - x, y), (x, 1 - y), (1 - x, 1 - y)]
    return x, y, c, others


def own_slot(w, dtype, me_arr):
    L, K, N = w.shape
    tk = _tile(K, 512)

    def body(me_ref, w_ref, o_ref):
        o_ref[...] = w_ref[...].astype(dtype)

    return pl.pallas_call(
        body, name="own_slot",
        grid_spec=pltpu.PrefetchScalarGridSpec(
            num_scalar_prefetch=1, grid=(L, K // tk),
            in_specs=[pl.BlockSpec((None, tk, N), lambda l, i, me: (l, i, 0))],
            out_specs=pl.BlockSpec((None, None, tk, N), lambda l, i, me: (me[0], l, i, 0))),
        out_shape=jax.ShapeDtypeStruct((N_CHIP, L, K, N), dtype),
        compiler_params=_params("arbitrary", "arbitrary"))(me_arr, w)


def gather_weight(buf):
    def body(w_ref, out_ref, send_sems, recv_sems):
        x, y, c, others = _place()
        sibling = (x, y, 1 - c)

        def copy(k, chip, layer, to):
            dst = out_ref.at[2 * chip[0] + chip[1], layer]
            return pltpu.make_async_remote_copy(
                src_ref=dst, dst_ref=dst, send_sem=send_sems.at[k],
                recv_sem=recv_sems.at[k], device_id=to, device_id_type=MESH)

        first = [copy(k, (x, y), c, (*chip, c)) for k, chip in enumerate(others)]
        for cp in first:
            cp.start()
        passed = [copy(3 + k, chip, c, sibling) for k, chip in enumerate(others)]
        for k, chip in enumerate(others):
            copy(k, chip, c, (x, y, c)).wait_recv()
            passed[k].start()
        for k, chip in enumerate(others):
            copy(3 + k, chip, 1 - c, (x, y, c)).wait_recv()
        for cp in first + passed:
            cp.wait_send()

    return pl.pallas_call(
        body, name="gather_weight", in_specs=[HBM], out_specs=HBM,
        out_shape=jax.ShapeDtypeStruct(buf.shape, buf.dtype), input_output_aliases={0: 0},
        scratch_shapes=[pltpu.SemaphoreType.DMA((6,)), pltpu.SemaphoreType.DMA((6,))],
    )(buf)


def swap_half_rows(g):
    K = g.shape[1]
    Kh = K // 2

    def body(g_ref, out_ref, send_sem, recv_sem):
        x, y, c, _ = _place()
        cp = pltpu.make_async_remote_copy(
            src_ref=g_ref.at[:, pl.ds((1 - c) * Kh, Kh), :], dst_ref=out_ref, send_sem=send_sem,
            recv_sem=recv_sem, device_id=(x, y, 1 - c), device_id_type=MESH)
        cp.start()
        cp.wait()

    return pl.pallas_call(
        body, name="swap_half_rows", in_specs=[HBM], out_specs=HBM,
        out_shape=jax.ShapeDtypeStruct((N_CHIP, Kh, g.shape[2]), g.dtype),
        scratch_shapes=[pltpu.SemaphoreType.DMA, pltpu.SemaphoreType.DMA])(g)


def chip_sum(g, got, c_arr):
    _, K, N = g.shape
    Kh = K // 2
    tr = _tile(Kh, 352 if Kh % 352 == 0 else 256)

    def body(c_ref, g_ref, got_ref, o_ref):
        o_ref[...] = (g_ref[...] + got_ref[...]).astype(BF16)

    nt = Kh // tr
    return pl.pallas_call(
        body, name="chip_sum",
        grid_spec=pltpu.PrefetchScalarGridSpec(
            num_scalar_prefetch=1, grid=(N_CHIP, nt),
            in_specs=[pl.BlockSpec((None, tr, N), lambda j, i, c: (j, c[0] * nt + i, 0)),
                      pl.BlockSpec((None, tr, N), lambda j, i, c: (j, i, 0))],
            out_specs=pl.BlockSpec((None, tr, N), lambda j, i, c: (j, i, 0))),
        out_shape=jax.ShapeDtypeStruct((N_CHIP, Kh, N), BF16),
        compiler_params=_params("arbitrary", "arbitrary"))(c_arr, g, got)


def scatter_chip_sums(cs):
    def body(cs_ref, out_ref, send_sems, recv_sems):
        x, y, c, others = _place()
        cps = [pltpu.make_async_remote_copy(
            src_ref=cs_ref.at[2 * chip[0] + chip[1]], dst_ref=out_ref.at[k], send_sem=send_sems.at[k],
            recv_sem=recv_sems.at[k], device_id=(*chip, c), device_id_type=MESH)
            for k, chip in enumerate(others)]
        for cp in cps:
            cp.start()
        for cp in cps:
            cp.wait()

    return pl.pallas_call(
        body, name="scatter_chip_sums", in_specs=[HBM], out_specs=HBM,
        out_shape=jax.ShapeDtypeStruct((3,) + cs.shape[1:], cs.dtype),
        scratch_shapes=[pltpu.SemaphoreType.DMA((3,)), pltpu.SemaphoreType.DMA((3,))])(cs)


def total_sum(cs, got, me_arr, c_arr, l, into):
    _, Kh, N = cs.shape
    tr = _tile(Kh, 352 if Kh % 352 == 0 else 256)
    nt = Kh // tr

    def body(me_ref, c_ref, cs_ref, got_ref, *rest):
        o_ref = rest[-1]
        o_ref[...] = ((cs_ref[...].astype(F32) + got_ref[0].astype(F32)) + got_ref[1].astype(F32)) + \
            got_ref[2].astype(F32)

    ins = [pl.BlockSpec((None, tr, N), lambda i, me, c: (me[0], i, 0)),
           pl.BlockSpec((3, tr, N), lambda i, me, c: (0, i, 0))]
    args = [me_arr, c_arr, cs, got]
    alias = {}
    if into is not None:
        ins.append(pl.BlockSpec(memory_space=pl.ANY))
        args.append(into)
        alias = {4: 0}
    return pl.pallas_call(
        body, name="total_sum",
        grid_spec=pltpu.PrefetchScalarGridSpec(
            num_scalar_prefetch=2, grid=(nt,), in_specs=ins,
            out_specs=pl.BlockSpec((None, tr, N), lambda i, me, c: (l, c[0] * nt + i, 0))),
        out_shape=jax.ShapeDtypeStruct((2, 2 * Kh, N), F32), input_output_aliases=alias,
        compiler_params=_params("arbitrary"))(*args)


def join_halves(buf):
    Kh = buf.shape[1] // 2

    def body(in_ref, out_ref, send_sem, recv_sem):
        x, y, c, _ = _place()
        mine = out_ref.at[:, pl.ds(c * Kh, Kh), :]
        cp = pltpu.make_async_remote_copy(
            src_ref=mine, dst_ref=mine, send_sem=send_sem, recv_sem=recv_sem,
            device_id=(x, y, 1 - c), device_id_type=MESH)
        cp.start()
        cp.wait()

    return pl.pallas_call(
        body, name="join_halves", in_specs=[HBM], out_specs=HBM,
        out_shape=jax.ShapeDtypeStruct(buf.shape, F32), input_output_aliases={0: 0},
        scratch_shapes=[pltpu.SemaphoreType.DMA, pltpu.SemaphoreType.DMA])(buf)


def allreduce_small(p):
    R = p.shape[0]

    def body(p_ref, out_ref, all_ref, send_sems, recv_sems, local_sem):
        x, y, c, others = _place()
        me, sibling = (x, y, c), (x, y, 1 - c)

        def rows(px, py, pc):
            return all_ref.at[pl.ds((4 * px + 2 * py + pc) * R, R), :]

        def copy(k, block, to, src=None):
            return pltpu.make_async_remote_copy(
                src_ref=rows(*block) if src is None else src, dst_ref=rows(*block), send_sem=send_sems.at[k],
                recv_sem=recv_sems.at[k], device_id=to, device_id_type=MESH)

        mine = pltpu.make_async_copy(p_ref, rows(*me), local_sem)
        mine.start()
        first = [copy(0, me, sibling, src=p_ref)]
        first += [copy(1 + j, me, (*chip, c), src=p_ref) for j, chip in enumerate(others)]
        for cp in first:
            cp.start()
        passed = [copy(4 + j, (*chip, c), sibling) for j, chip in enumerate(others)]
        for j, chip in enumerate(others):
            copy(1 + j, (*chip, c), me).wait_recv()
            passed[j].start()
        copy(0, sibling, me).wait_recv()
        for j, chip in enumerate(others):
            copy(4 + j, (*chip, 1 - c), me).wait_recv()
        for cp in first + passed:
            cp.wait_send()
        mine.wait()
        total = all_ref[pl.ds(0, R), :]
        for d in range(1, N_DEV):
            total = total + all_ref[pl.ds(d * R, R), :]
        out_ref[...] = total

    vmem = pl.BlockSpec(memory_space=pltpu.VMEM)
    return pl.pallas_call(
        body, name="allreduce_small", in_specs=[vmem], out_specs=vmem,
        out_shape=jax.ShapeDtypeStruct((R, 128), F32),
        scratch_shapes=[pltpu.VMEM((N_DEV * R, 128), F32), pltpu.SemaphoreType.DMA((7,)),
                        pltpu.SemaphoreType.DMA((7,)), pltpu.SemaphoreType.DMA],
        compiler_params=pltpu.CompilerParams(vmem_limit_bytes=VMEM_LIMIT_V7X))(p)


def _ffn_fwd(x, g, wg, wu, wd, l):
    h = rmsnorm_fwd(x, g)
    a, b, y = ffn_up(h, wg, wu, l)
    return residual_matmul(x, y, wd, l, 0.5, True), (x, h, a, b, y)


def _ffn_bwd(dxo, saved, g, wg, wu, wd, l):
    x, h, a, b, y = saved
    D = x.shape[1]
    F = wg.shape[-1]
    da, db = ffn_bwd_hidden(dxo, wd, a, b, l)
    g_wd = weight_grad(y, dxo, F, D, True, False, b_scale=0.5)
    g_wg = weight_grad(h, da, D, F, False, True)
    g_wu = weight_grad(h, db, D, F, False, True)
    dx, dg = norm_bwd_matmul(x, g, dxo, [(da, wg), (db, wu)], l, True)
    return dx, dg, g_wg, g_wu, g_wd


def kernel(x, norm_ffn1, ffn1_gate, ffn1_up, ffn1_down, norm_mix, w_in, conv_w, conv_b, rg_w_a, rg_b_a, rg_w_x, rg_b_x, rg_lambda, pool_w, pool_scale, w_out, norm_ffn2, ffn2_gate, ffn2_up, ffn2_down, norm_final, loss_target, m_norm_ffn1, m_ffn1_gate, m_ffn1_up, m_ffn1_down, m_norm_mix, m_w_in, m_conv_w, m_conv_b, m_rg_w_a, m_rg_b_a, m_rg_w_x, m_rg_b_x, m_rg_lambda, m_pool_w, m_pool_scale, m_w_out, m_norm_ffn2, m_ffn2_gate, m_ffn2_up, m_ffn2_down, m_norm_final, v_norm_ffn1, v_ffn1_gate, v_ffn1_up, v_ffn1_down, v_norm_mix, v_w_in, v_conv_w, v_conv_b, v_rg_w_a, v_rg_b_a, v_rg_w_x, v_rg_b_x, v_rg_lambda, v_pool_w, v_pool_scale, v_w_out, v_norm_ffn2, v_ffn2_gate, v_ffn2_up, v_ffn2_down, v_norm_final):
    given = dict(locals())
    W = {n: given[n] for n in WEIGHTS}
    M = {n: given["m_" + n] for n in WEIGHTS}
    V = {n: given["v_" + n] for n in WEIGHTS}
    depth = norm_ffn1.shape[0]
    S, D = x.shape[1], x.shape[2]
    xs = x.reshape(S, D)
    target = loss_target.reshape(S, D)
    C = conv_b.shape[-1]
    n_heads = (D // 2) // HEAD
    cx, cy, cc = lax.axis_index("x"), lax.axis_index("y"), lax.axis_index("c")
    me = 2 * cx + cy
    me_arr = jnp.reshape(me, (1,)).astype(jnp.int32)
    c_arr = jnp.reshape(cc, (1,)).astype(jnp.int32)

    Wg = {n: gather_weight(own_slot(W[n], BF16, me_arr)) for n in BIG}
    conv_all = gather_weight(own_slot(conv_w, F32, me_arr))
    conv_full = jnp.transpose(conv_all, (1, 2, 0, 3)).reshape(depth, CONV_WIDTH, C)

    row = lambda a, l: a[l].reshape(1, -1)

    saved = []
    cur = xs
    for l in range(depth):
        cur, s1 = _ffn_fwd(cur, row(norm_ffn1, l), Wg['ffn1_gate'], Wg['ffn1_up'], Wg['ffn1_down'], l)
        x1 = cur
        hm = rmsnorm_fwd(x1, row(norm_mix, l))
        z, zb = mix_in(hm, Wg['w_in'], l)
        att, ltot = attn_fwd(zb, n_heads)
        rnn_p = (conv_full[l], row(conv_b, l), rg_w_a[l], row(rg_b_a, l), rg_w_x[l], row(rg_b_x, l),
                 row(rg_lambda, l))
        rnn, hs = rnn_fwd(z, *rnn_p)
        pool = pool_fwd(z, pool_w[l], row(pool_scale, l))
        cat = jnp.concatenate([att, rnn, pool], axis=1)
        cur = residual_matmul(x1, cat, Wg['w_out'], l, 1.0, False)
        x2 = cur
        cur, s2 = _ffn_fwd(cur, row(norm_ffn2, l), Wg['ffn2_gate'], Wg['ffn2_up'], Wg['ffn2_down'], l)
        saved.append((s1, (x1, hm, z, zb, ltot, hs, cat, rnn_p), s2))

    loss_cols, dx, g_norm_final = final_loss(cur, norm_final.reshape(1, D), target)

    G = {n: [None] * depth for n in WEIGHTS if n != 'norm_final'}
    for l in reversed(range(depth)):
        s1, (x1, hm, z, zb, ltot, hs, cat, rnn_p), s2 = saved[l]
        dx, dg, g_wg, g_wu, g_wd = _ffn_bwd(dx, s2, row(norm_ffn2, l), Wg['ffn2_gate'], Wg['ffn2_up'],
                                            Wg['ffn2_down'], l)
        G['norm_ffn2'][l], G['ffn2_gate'][l], G['ffn2_up'][l], G['ffn2_down'][l] = dg, g_wg, g_wu, g_wd
        dc = mix_out_bwd(dx, Wg['w_out'], l)
        G['w_out'][l] = weight_grad(cat, dx, C, D, False, False)
        dq, dk, dv = attn_bwd(zb, ltot, dc, n_heads)
        (dxg, dxr, G['conv_w'][l], G['conv_b'][l], G['rg_w_a'][l], G['rg_b_a'][l], G['rg_w_x'][l],
         G['rg_b_x'][l], G['rg_lambda'][l]) = rnn_bwd(z, hs, dc, *rnn_p)
        dxp, G['pool_w'][l], G['pool_scale'][l] = pool_bwd(z, dc, pool_w[l], row(pool_scale, l))
        dz = jnp.concatenate([dq.astype(BF16), dk.astype(BF16), dv.astype(BF16), dxg, dxr, dxp], axis=1)
        Nb = dz.shape[1] // N_CHIP
        G['w_in'][l] = weight_grad(hm, dz, D, Nb, False, False)
        dx, G['norm_mix'][l] = norm_bwd_matmul(x1, row(norm_mix, l), dx, [(dz, Wg['w_in'])], l, False)
        dx, dg, g_wg, g_wu, g_wd = _ffn_bwd(dx, s1, row(norm_ffn1, l), Wg['ffn1_gate'], Wg['ffn1_up'],
                                            Wg['ffn1_down'], l)
        G['norm_ffn1'][l], G['ffn1_gate'][l], G['ffn1_up'][l], G['ffn1_down'][l] = dg, g_wg, g_wu, g_wd
    grad_x = dx.reshape(x.shape)

    outs = {}
    for n in BIG:
        halves = None
        for l in range(depth):
            g = G[n][l]
            cs = chip_sum(g, swap_half_rows(g), c_arr)
            halves = total_sum(cs, scatter_chip_sums(cs), me_arr, c_arr, l, halves)
        g_red = join_halves(halves)
        two_d = (-1, g_red.shape[-1])
        d, m2, v2 = adamw(W[n].reshape(two_d), g_red.reshape(two_d), M[n].reshape(two_d), V[n].reshape(two_d))
        outs[n] = (g_red, d.reshape(W[n].shape), m2.reshape(W[n].shape), v2.reshape(W[n].shape))

    loss_part = 0.5 * jnp.sum(loss_cols) / D
    parts = [jnp.stack(G[n]).reshape(-1) for n in SMALL if n != 'norm_final']
    parts += [g_norm_final.reshape(-1), jnp.reshape(loss_part, (1,))]
    sizes = [p.shape[0] for p in parts]
    total = sum(sizes)
    R = -(-total // 1024) * 8
    packed = jnp.concatenate(parts + [jnp.zeros((R * 128 - total,), F32)]).reshape(R, 128)
    red = allreduce_small(packed).reshape(-1)
    offs = [sum(sizes[:i]) for i in range(len(sizes))]
    small_names = [n for n in SMALL if n != 'norm_final'] + ['norm_final']
    small_g = {}
    for n, o, sz in zip(small_names, offs, sizes):
        if n == 'conv_w':
            full = red[o:o + sz].reshape(depth, CONV_WIDTH, C)
            small_g[n] = lax.dynamic_slice_in_dim(full, me * (C // N_CHIP), C // N_CHIP, axis=2)
        else:
            small_g[n] = red[o:o + sz].reshape(W[n].shape)
    loss = red[offs[-1]]

    def pack(d):
        flat = jnp.concatenate([d[n].reshape(-1) for n in small_names])
        rows = -(-flat.shape[0] // 1024) * 8
        return jnp.concatenate([flat, jnp.ones((rows * 128 - flat.shape[0],), F32)]).reshape(rows, 128)

    d_s, m_s, v_s = adamw(pack(W), pack(small_g), pack(M), pack(V))
    o = 0
    for n in small_names:
        sz = W[n].size
        outs[n] = (small_g[n],) + tuple(a.reshape(-1)[o:o + sz].reshape(W[n].shape) for a in (d_s, m_s, v_s))
        o += sz

    return (loss, grad_x, *[outs[n][0] for n in WEIGHTS], *[outs[n][1] for n in WEIGHTS],
            *[outs[n][2] for n in WEIGHTS], *[outs[n][3] for n in WEIGHTS])
```

```python
import functools

import jax
import jax.numpy as jnp
from jax import lax
from jax.experimental import pallas as pl
from jax.experimental.pallas import tpu as pltpu

F32 = jnp.float32
BF16 = jnp.bfloat16
MESH = pl.DeviceIdType.MESH

HEAD = 128
ATTN_TQ, ATTN_TK = 1024, 256
ATTN_TQ_FWD = 2048
POOL_WINDOWS = (2, 4, 8, 16)
CONV_WIDTH = 4
RG_C = 8.0
NORM_EPS = 1e-6
N_CHIP = 4
N_DEV = 8
ADAM_LR, ADAM_B1, ADAM_B2, ADAM_EPS, ADAM_WD, ADAM_STEP = 0.001, 0.9, 0.999, 1e-08, 0.01, 10
VMEM_LIMIT_V7X = 56 * 2 ** 20

WEIGHTS = ['norm_ffn1', 'ffn1_gate', 'ffn1_up', 'ffn1_down', 'norm_mix', 'w_in', 'conv_w', 'conv_b', 'rg_w_a',
           'rg_b_a', 'rg_w_x', 'rg_b_x', 'rg_lambda', 'pool_w', 'pool_scale', 'w_out', 'norm_ffn2', 'ffn2_gate',
           'ffn2_up', 'ffn2_down', 'norm_final']
BIG = ['ffn1_gate', 'ffn1_up', 'ffn1_down', 'w_in', 'w_out', 'ffn2_gate', 'ffn2_up', 'ffn2_down']
SMALL = [n for n in WEIGHTS if n not in BIG]


def _params(*sem):
    return pltpu.CompilerParams(dimension_semantics=sem, vmem_limit_bytes=VMEM_LIMIT_V7X)


def _tile(n, pref):
    t = min(n, pref)
    while n % t:
        t //= 2
    return t


def _dot(a, b):
    return jnp.dot(a, b, preferred_element_type=F32)


def _dot_nt(a, b):
    return lax.dot_general(a, b, (((1,), (1,)), ((), ())), preferred_element_type=F32)


def _dot_tn(a, b):
    return lax.dot_general(a, b, (((0,), (0,)), ((), ())), preferred_element_type=F32)


def _sigmoid(x):
    return 1.0 / (1.0 + jnp.exp(-x))


def _gelu(x):
    return 0.5 * x * (1.0 + jnp.tanh(0.7978845608028654 * (x + 0.044715 * x * x * x)))


def _expm1(x):
    series = x * (1.0 + x * 0.5 * (1.0 + x * (1.0 / 3.0) * (1.0 + x * 0.25 * (1.0 + x * 0.2))))
    return jnp.where(jnp.abs(x) < 0.1, series, jnp.exp(x) - 1.0)


def _neg_softplus(z):
    return -(jnp.maximum(z, 0.0) + jnp.log(1.0 + jnp.exp(-jnp.abs(z))))


def _split_bf16(x):
    hi = x.astype(BF16)
    return hi, (x - hi.astype(F32)).astype(BF16)


def _wspec(k, n, l, jpos):
    return pl.BlockSpec((None, None, k, n), lambda *g: (g[jpos], l, 0, 0))


HBM = pl.BlockSpec(memory_space=pltpu.HBM)


def _place():
    x, y, c = lax.axis_index("x"), lax.axis_index("y"), lax.axis_index("c")
    others = [(1 - x, y), (x, 1 - y), (1 - x, 1 - y)]
    return x, y, c, others


def _scatter_job(cs):
    def copies(ins, outs, sems):
        x, y, c, others = _place()
        return [pltpu.make_async_remote_copy(
            src_ref=ins[0].at[2 * chip[0] + chip[1]], dst_ref=outs[0].at[k], send_sem=sems[0].at[k],
            recv_sem=sems[1].at[k], device_id=(*chip, c), device_id_type=MESH) for k, chip in enumerate(others)]

    def start(ins, outs, sems):
        for cp in copies(ins, outs, sems):
            cp.start()

    def finish(ins, outs, sems):
        for cp in copies(ins, outs, sems):
            cp.wait()

    return dict(ins=[cs], outs=[jax.ShapeDtypeStruct((3,) + cs.shape[1:], cs.dtype)], alias={},
                sems=[pltpu.SemaphoreType.DMA((3,)), pltpu.SemaphoreType.DMA((3,))], start=start, finish=finish)


def _gather_job(buf):
    def copy(outs, sems, k, chip, layer, to):
        dst = outs[0].at[2 * chip[0] + chip[1], layer]
        return pltpu.make_async_remote_copy(src_ref=dst, dst_ref=dst, send_sem=sems[0].at[k], recv_sem=sems[1].at[k],
                                            device_id=to, device_id_type=MESH)

    def start(ins, outs, sems):
        x, y, c, others = _place()
        for k, chip in enumerate(others):
            copy(outs, sems, k, (x, y), c, (*chip, c)).start()

    def finish(ins, outs, sems):
        x, y, c, others = _place()
        me, sibling = (x, y, c), (x, y, 1 - c)
        for k, chip in enumerate(others):
            copy(outs, sems, k, chip, c, me).wait_recv()
            copy(outs, sems, 3 + k, chip, c, sibling).start()
        for k, chip in enumerate(others):
            copy(outs, sems, 3 + k, chip, 1 - c, me).wait_recv()
        for k in range(6):
            copy(outs, sems, k, (x, y), c, me).wait_send()

    return dict(ins=[buf], outs=[jax.ShapeDtypeStruct(buf.shape, buf.dtype)], alias={0: 0},
                sems=[pltpu.SemaphoreType.DMA((6,)), pltpu.SemaphoreType.DMA((6,))], start=start, finish=finish)


def _carried_call(body, jobs, name, grid, in_specs, out_specs, out_shape, scratch_shapes, args, sem):
    n_in, n_out, n_scr = len(in_specs), len(out_specs), len(scratch_shapes)
    j_in = [a for j in jobs for a in j['ins']]
    j_out = [o for j in jobs for o in j['outs']]
    j_sem = [s for j in jobs for s in j['sems']]

    def wrapped(*refs):
        own_in, ji = refs[:n_in], list(refs[n_in:n_in + len(j_in)])
        rest = refs[n_in + len(j_in):]
        own_out, jo = rest[:n_out], list(rest[n_out:n_out + len(j_out)])
        rest = rest[n_out + len(j_out):]
        own_scr, js = rest[:n_scr], list(rest[n_scr:])
        per_job = []
        for j in jobs:
            per_job.append((j, [ji.pop(0) for _ in j['ins']], [jo.pop(0) for _ in j['outs']],
                            [js.pop(0) for _ in j['sems']]))
        ids = [pl.program_id(d) for d in range(len(grid))]
        first = functools.reduce(lambda a, b: a & b, [i == 0 for i in ids])
        last = functools.reduce(lambda a, b: a & b, [i == g - 1 for i, g in zip(ids, grid)])

        if jobs:
            @pl.when(first)
            def _():
                for j, a, o, s in per_job:
                    j['start'](a, o, s)

        body(*own_in, *own_out, *own_scr)

        if jobs:
            @pl.when(last)
            def _():
                for j, a, o, s in per_job:
                    j['finish'](a, o, s)

    alias, off_in, off_out = {}, n_in, n_out
    for j in jobs:
        for li, lo in j['alias'].items():
            alias[off_in + li] = off_out + lo
        off_in += len(j['ins'])
        off_out += len(j['outs'])
    res = pl.pallas_call(
        wrapped, name=name, grid=grid, in_specs=list(in_specs) + [HBM] * len(j_in),
        out_specs=list(out_specs) + [HBM] * len(j_out), out_shape=list(out_shape) + j_out,
        scratch_shapes=list(scratch_shapes) + j_sem, input_output_aliases=alias,
        compiler_params=_params(*sem))(*args, *j_in)
    own, rest = list(res[:n_out]), list(res[n_out:])
    return own, [[rest.pop(0) for _ in j['outs']] for j in jobs]


def _run_jobs(jobs, name):
    def body(*refs):
        ins, rest = list(refs[:n_in]), list(refs[n_in:])
        outs, sems = rest[:n_out], rest[n_out:]
        per_job = [(j, [ins.pop(0) for _ in j['ins']], [outs.pop(0) for _ in j['outs']],
                    [sems.pop(0) for _ in j['sems']]) for j in jobs]
        for j, a, o, s in per_job:
            j['start'](a, o, s)
        for j, a, o, s in per_job:
            j['finish'](a, o, s)

    j_in = [a for j in jobs for a in j['ins']]
    j_out = [o for j in jobs for o in j['outs']]
    n_in, n_out = len(j_in), len(j_out)
    alias, off_in, off_out = {}, 0, 0
    for j in jobs:
        for li, lo in j['alias'].items():
            alias[off_in + li] = off_out + lo
        off_in += len(j['ins'])
        off_out += len(j['outs'])
    res = list(pl.pallas_call(
        body, name=name, in_specs=[HBM] * n_in, out_specs=[HBM] * n_out, out_shape=j_out,
        scratch_shapes=[s for j in jobs for s in j['sems']], input_output_aliases=alias)(*j_in))
    return [[res.pop(0) for _ in j['outs']] for j in jobs]


def rmsnorm_fwd(x, g):
    S, D = x.shape
    tm = _tile(S, 512)

    def body(x_ref, g_ref, h_ref):
        xv = x_ref[...]
        rstd = lax.rsqrt(jnp.mean(xv * xv, axis=-1, keepdims=True) + NORM_EPS)
        h_ref[...] = (xv * rstd * g_ref[...]).astype(BF16)

    return pl.pallas_call(
        body, name="rmsnorm_fwd", grid=(S // tm,),
        in_specs=[pl.BlockSpec((tm, D), lambda i: (i, 0)), pl.BlockSpec((1, D), lambda i: (0, 0))],
        out_specs=pl.BlockSpec((tm, D), lambda i: (i, 0)),
        out_shape=jax.ShapeDtypeStruct((S, D), BF16), compiler_params=_params("arbitrary"))(x, g)


def ffn_up(h, wg, wu, l, jobs=()):
    S, D = h.shape
    F = wg.shape[-1]
    tm = _tile(S, 512)

    def body(h_ref, wg_ref, wu_ref, a_ref, b_ref, y_ref):
        hv = h_ref[...]
        a = _dot(hv, wg_ref[...])
        b = _dot(hv, wu_ref[...])
        a_ref[...] = a.astype(BF16)
        b_ref[...] = b.astype(BF16)
        y_ref[...] = (a * _sigmoid(a) * b).astype(BF16)

    out = pl.BlockSpec((None, tm, F), lambda j, i: (j, i, 0))
    shp = jax.ShapeDtypeStruct((N_CHIP, S, F), BF16)
    return _carried_call(
        body, jobs, "ffn_up", (N_CHIP, S // tm),
        [pl.BlockSpec((tm, D), lambda j, i: (i, 0)), _wspec(D, F, l, 0), _wspec(D, F, l, 0)],
        [out, out, out], [shp, shp, shp], [], (h, wg, wu), ("arbitrary", "arbitrary"))


def residual_matmul(x, a, w, l, scale, a_blocked):
    S, D = x.shape
    Kb = w.shape[2]
    tm = _tile(S, 512)

    def body(x_ref, a_ref, w_ref, o_ref, acc):
        j = pl.program_id(1)

        @pl.when(j == 0)
        def _():
            acc[...] = jnp.zeros_like(acc)

        acc[...] += _dot(a_ref[...], w_ref[...])

        @pl.when(j == N_CHIP - 1)
        def _():
            o_ref[...] = x_ref[...] + scale * acc[...]

    a_spec = (pl.BlockSpec((None, tm, Kb), lambda i, j: (j, i, 0)) if a_blocked
              else pl.BlockSpec((tm, Kb), lambda i, j: (i, j)))
    return pl.pallas_call(
        body, name="residual_matmul", grid=(S // tm, N_CHIP),
        in_specs=[pl.BlockSpec((tm, D), lambda i, j: (i, 0)), a_spec, _wspec(Kb, D, l, 1)],
        out_specs=pl.BlockSpec((tm, D), lambda i, j: (i, 0)),
        out_shape=jax.ShapeDtypeStruct((S, D), F32),
        scratch_shapes=[pltpu.VMEM((tm, D), F32)],
        compiler_params=_params("arbitrary", "arbitrary"))(x, a, w)


def norm_bwd_matmul(x, g, dxo, pairs, l, g_blocked):
    S, D = x.shape
    Nb = pairs[0][1].shape[-1]
    tm = _tile(S, 256)
    n_p = len(pairs)

    def body(*refs):
        x_ref, g_ref, dxo_ref = refs[:3]
        g_refs = refs[3:3 + n_p]
        w_refs = refs[3 + n_p:3 + 2 * n_p]
        dx_ref, dg_ref, acc = refs[3 + 2 * n_p:]
        i, j = pl.program_id(0), pl.program_id(1)

        @pl.when(j == 0)
        def _():
            acc[...] = jnp.zeros_like(acc)

        @pl.when((i == 0) & (j == 0))
        def _():
            dg_ref[...] = jnp.zeros_like(dg_ref)

        for g_r, w_r in zip(g_refs, w_refs):
            acc[...] += _dot_nt(g_r[...], w_r[...])

        @pl.when(j == N_CHIP - 1)
        def _():
            xv = x_ref[...]
            rstd = lax.rsqrt(jnp.mean(xv * xv, axis=-1, keepdims=True) + NORM_EPS)
            xhat = xv * rstd
            dh = acc[...]
            gd = dh * g_ref[...]
            dx_ref[...] = dxo_ref[...] + rstd * (gd - xhat * jnp.mean(gd * xhat, axis=-1, keepdims=True))
            dg_ref[...] += jnp.sum(dh * xhat, axis=0, keepdims=True)

    row = pl.BlockSpec((tm, D), lambda i, j: (i, 0))
    vec = pl.BlockSpec((1, D), lambda i, j: (0, 0))
    g_spec = (pl.BlockSpec((None, tm, Nb), lambda i, j: (j, i, 0)) if g_blocked
              else pl.BlockSpec((tm, Nb), lambda i, j: (i, j)))
    return pl.pallas_call(
        body, name="norm_bwd_matmul", grid=(S // tm, N_CHIP),
        in_specs=[row, vec, row] + [g_spec] * n_p + [_wspec(D, Nb, l, 1)] * n_p,
        out_specs=[row, vec],
        out_shape=[jax.ShapeDtypeStruct((S, D), F32), jax.ShapeDtypeStruct((1, D), F32)],
        scratch_shapes=[pltpu.VMEM((tm, D), F32)],
        compiler_params=_params("arbitrary", "arbitrary"))(
            x, g, dxo, *[p[0] for p in pairs], *[p[1] for p in pairs])


def ffn_bwd_hidden(dxo, wd, a, b, l):
    S, D = dxo.shape
    F = wd.shape[2]
    tm = _tile(S, 512)

    def body(dxo_ref, wd_ref, a_ref, b_ref, da_ref, db_ref):
        dy = 0.5 * _dot_nt(dxo_ref[...].astype(BF16), wd_ref[...])
        av = a_ref[...].astype(F32)
        bv = b_ref[...].astype(F32)
        s = _sigmoid(av)
        da_ref[...] = (dy * bv * (s * (1.0 + av * (1.0 - s)))).astype(BF16)
        db_ref[...] = (dy * (av * s)).astype(BF16)

    blk = pl.BlockSpec((None, tm, F), lambda j, i: (j, i, 0))
    shp = jax.ShapeDtypeStruct((N_CHIP, S, F), BF16)
    return pl.pallas_call(
        body, name="ffn_bwd_hidden", grid=(N_CHIP, S // tm),
        in_specs=[pl.BlockSpec((tm, D), lambda j, i: (i, 0)), _wspec(F, D, l, 0), blk, blk],
        out_specs=[blk, blk], out_shape=[shp, shp],
        compiler_params=_params("arbitrary", "arbitrary"))(dxo, wd, a, b)


def weight_grad(a, b, m, n, a_blocked, b_blocked, b_scale=1.0):
    S = a.shape[-2]
    ts = _tile(S, 1024)
    tm = m if m % 256 else _tile(m, 1024)
    tn = n if n % 256 else _tile(n, 1024)
    if m * n * 4 <= 6 * 2 ** 20:
        tm, tn = m, n

    def spec(arr, blocked, width, t, pos):
        if blocked:
            return pl.BlockSpec((None, ts, t), lambda j, mi, ni, si: (j, si, (mi, ni)[pos]))
        if arr.shape[-1] == width:
            return pl.BlockSpec((ts, t), lambda j, mi, ni, si: (si, (mi, ni)[pos]))
        nblk = width // t
        return pl.BlockSpec((ts, t), lambda j, mi, ni, si: (si, j * nblk + (mi, ni)[pos]))

    def body(a_ref, b_ref, o_ref):
        @pl.when(pl.program_id(3) == 0)
        def _():
            o_ref[...] = jnp.zeros_like(o_ref)

        bv = b_ref[...]
        if b_scale != 1.0:
            bv = b_scale * bv
        o_ref[...] += _dot_tn(a_ref[...], bv.astype(BF16))

    return pl.pallas_call(
        body, name="weight_grad", grid=(N_CHIP, m // tm, n // tn, S // ts),
        in_specs=[spec(a, a_blocked, m, tm, 0), spec(b, b_blocked, n, tn, 1)],
        out_specs=pl.BlockSpec((None, tm, tn), lambda j, mi, ni, si: (j, mi, ni)),
        out_shape=jax.ShapeDtypeStruct((N_CHIP, m, n), F32),
        compiler_params=_params("arbitrary", "arbitrary", "arbitrary", "arbitrary"))(a, b)


def mix_in(h, w_in, l):
    S, D = h.shape
    Nb = w_in.shape[-1]
    tm = _tile(S, 512)

    def body(h_ref, w_ref, z_ref, zb_ref):
        z = _dot(h_ref[...], w_ref[...])
        z_ref[...] = z
        zb_ref[...] = z.astype(BF16)

    out = pl.BlockSpec((tm, Nb), lambda j, i: (i, j))
    return pl.pallas_call(
        body, name="mix_in", grid=(N_CHIP, S // tm),
        in_specs=[pl.BlockSpec((tm, D), lambda j, i: (i, 0)), _wspec(D, Nb, l, 0)],
        out_specs=[out, out],
        out_shape=[jax.ShapeDtypeStruct((S, N_CHIP * Nb), F32), jax.ShapeDtypeStruct((S, N_CHIP * Nb), BF16)],
        compiler_params=_params("arbitrary", "arbitrary"))(h, w_in)


def mix_out_bwd(dx, w_out, l):
    S, D = dx.shape
    Kb = w_out.shape[2]
    tm = _tile(S, 512)

    def body(dx_ref, w_ref, dc_ref):
        dc_ref[...] = _dot_nt(dx_ref[...].astype(BF16), w_ref[...])

    return pl.pallas_call(
        body, name="mix_out_bwd", grid=(N_CHIP, S // tm),
        in_specs=[pl.BlockSpec((tm, D), lambda j, i: (i, 0)), _wspec(Kb, D, l, 0)],
        out_specs=pl.BlockSpec((tm, Kb), lambda j, i: (i, j)),
        out_shape=jax.ShapeDtypeStruct((S, N_CHIP * Kb), F32),
        compiler_params=_params("arbitrary", "arbitrary"))(dx, w_out)


def final_loss(x, g, target):
    S, D = x.shape
    tm = _tile(S, 256)

    def body(x_ref, g_ref, t_ref, loss_ref, dx_ref, dg_ref):
        @pl.when(pl.program_id(0) == 0)
        def _():
            loss_ref[...] = jnp.zeros_like(loss_ref)
            dg_ref[...] = jnp.zeros_like(dg_ref)

        xv = x_ref[...]
        rstd = lax.rsqrt(jnp.mean(xv * xv, axis=-1, keepdims=True) + NORM_EPS)
        xhat = xv * rstd
        err = xhat * g_ref[...] - t_ref[...]
        loss_ref[...] += jnp.sum(err * err, axis=0, keepdims=True)
        dy = err * (1.0 / D)
        gd = dy * g_ref[...]
        dx_ref[...] = rstd * (gd - xhat * jnp.mean(gd * xhat, axis=-1, keepdims=True))
        dg_ref[...] += jnp.sum(dy * xhat, axis=0, keepdims=True)

    row = pl.BlockSpec((tm, D), lambda i: (i, 0))
    vec = pl.BlockSpec((1, D), lambda i: (0, 0))
    return pl.pallas_call(
        body, name="final_loss", grid=(S // tm,), in_specs=[row, vec, row], out_specs=[vec, row, vec],
        out_shape=[jax.ShapeDtypeStruct((1, D), F32), jax.ShapeDtypeStruct((S, D), F32),
                   jax.ShapeDtypeStruct((1, D), F32)],
        compiler_params=_params("arbitrary"))(x, g, target)


def _tri(T, strict_lower):
    row = lax.broadcasted_iota(jnp.int32, (T, T), 0)
    col = lax.broadcasted_iota(jnp.int32, (T, T), 1)
    return jnp.where((row > col) if strict_lower else (row < col), 1.0, 0.0).astype(BF16)


def _causal(rows, cols):
    return lax.broadcasted_iota(jnp.int32, (rows, cols), 1) < lax.broadcasted_iota(jnp.int32, (rows, cols), 0)


def attn_fwd(zb, n_heads, jobs=()):
    S = zb.shape[0]
    TQ = _tile(S, ATTN_TQ_FWD)
    TK = min(ATTN_TK, TQ)
    R = TQ // TK
    scale = HEAD ** -0.5

    def body(q_ref, k_ref, v_ref, o_ref, l_ref):
        i = pl.program_id(1)
        q = q_ref[...]
        later = _tri(TK, True)

        def tile(qr, k, v, carry, acc, mask):
            z = _dot_nt(qr, k) * scale
            lf = _neg_softplus(z)
            if mask is not None:
                lf = jnp.where(mask, lf, 0.0)
            hi, lo = _split_bf16(lf)
            after = _dot(hi, later) + _dot(lo, later)
            w = jnp.exp(z + lf + after + carry)
            if mask is not None:
                w = jnp.where(mask, w, 0.0)
            acc = acc + _dot(w.astype(BF16), v)
            carry = carry + jnp.sum(lf, axis=1, keepdims=True)
            return carry, acc

        def kv(kb):
            start = pl.multiple_of(kb * TK, TK)
            return k_ref[pl.ds(start, TK), :], v_ref[pl.ds(start, TK), :]

        carry, acc = jnp.zeros((TQ, 1), F32), jnp.zeros((TQ, HEAD), F32)
        for r in reversed(range(R)):
            k, v = kv(i * R + r)
            c_r, a_r = tile(q[r * TK:], k, v, carry[r * TK:], acc[r * TK:], _causal(TQ - r * TK, TK))
            carry = c_r if r == 0 else jnp.concatenate([carry[:r * TK], c_r], axis=0)
            acc = a_r if r == 0 else jnp.concatenate([acc[:r * TK], a_r], axis=0)
        carry, acc = lax.fori_loop(0, i * R, lambda n, ca: tile(q, *kv(i * R - 1 - n), ca[0], ca[1], None),
                                   (carry, acc))
        o_ref[...] = acc.astype(BF16)
        l_ref[...] = carry

    H = n_heads
    return _carried_call(
        body, jobs, "attn_fwd", (H, S // TQ),
        [pl.BlockSpec((TQ, HEAD), lambda h, i: (i, h)),
         pl.BlockSpec((S, HEAD), lambda h, i: (0, H + h)),
         pl.BlockSpec((S, HEAD), lambda h, i: (0, 2 * H + h))],
        [pl.BlockSpec((TQ, HEAD), lambda h, i: (i, h)), pl.BlockSpec((None, TQ, 1), lambda h, i: (h, i, 0))],
        [jax.ShapeDtypeStruct((S, H * HEAD), BF16), jax.ShapeDtypeStruct((H, S, 1), F32)],
        [], (zb, zb, zb), ("arbitrary", "arbitrary"))


def attn_bwd(zb, ltot, dc, n_heads, jobs=()):
    S = zb.shape[0]
    TQ = _tile(S, ATTN_TQ)
    TK = min(ATTN_TK, TQ)
    R = TQ // TK
    scale = HEAD ** -0.5

    def body(q_ref, k_ref, v_ref, l_ref, do_ref, dq_ref, dk_ref, dv_ref):
        i = pl.program_id(1)

        @pl.when(i == 0)
        def _():
            dk_ref[...] = jnp.zeros_like(dk_ref)
            dv_ref[...] = jnp.zeros_like(dv_ref)

        q = q_ref[...]
        dob = do_ref[...].astype(BF16)
        ltot_rows = l_ref[...]
        earlier = _tri(TK, False)

        def tile(kb, r0, c_lf, c_g, dq, mask):
            start = pl.multiple_of(kb * TK, TK)
            k = k_ref[pl.ds(start, TK), :]
            v = v_ref[pl.ds(start, TK), :]
            z = _dot_nt(q[r0:], k) * scale
            lf = _neg_softplus(z)
            if mask is not None:
                lf = jnp.where(mask, lf, 0.0)
            hi, lo = _split_bf16(lf)
            upto = _dot(hi, earlier) + _dot(lo, earlier) + lf + c_lf
            w = jnp.exp(z + lf + (ltot_rows[r0:] - upto))
            if mask is not None:
                w = jnp.where(mask, w, 0.0)
            g = w * _dot_nt(dob[r0:], v)
            before = _dot(g.astype(BF16), earlier) + c_g
            sig = jnp.exp(z + lf)
            dz = g * (1.0 - sig) - before * sig
            if mask is not None:
                dz = jnp.where(mask, dz, 0.0)
            dzb = dz.astype(BF16)
            dv_ref[pl.ds(start, TK), :] += _dot_tn(w.astype(BF16), dob[r0:])
            dk_ref[pl.ds(start, TK), :] += scale * _dot_tn(dzb, q[r0:])
            dq = dq + scale * _dot(dzb, k)
            return (c_lf + jnp.sum(lf, axis=1, keepdims=True), c_g + jnp.sum(g, axis=1, keepdims=True), dq)

        zero = jnp.zeros((TQ, 1), F32)
        c_lf, c_g, dq = lax.fori_loop(0, i * R, lambda kb, c: tile(kb, 0, c[0], c[1], c[2], None),
                                      (zero, zero, jnp.zeros((TQ, HEAD), F32)))
        for r in range(R):
            r0 = r * TK
            c_r, g_r, dq_r = tile(i * R + r, r0, c_lf[r0:], c_g[r0:], dq[r0:], _causal(TQ - r0, TK))
            if r == 0:
                c_lf, c_g, dq = c_r, g_r, dq_r
            else:
                c_lf = jnp.concatenate([c_lf[:r0], c_r], axis=0)
                c_g = jnp.concatenate([c_g[:r0], g_r], axis=0)
                dq = jnp.concatenate([dq[:r0], dq_r], axis=0)
        dq_ref[...] = dq

    H = n_heads
    full = pl.BlockSpec((S, HEAD), lambda h, i: (0, h))
    blk = pl.BlockSpec((TQ, HEAD), lambda h, i: (i, h))
    shp = jax.ShapeDtypeStruct((S, H * HEAD), F32)
    return _carried_call(
        body, jobs, "attn_bwd", (H, S // TQ),
        [blk, pl.BlockSpec((S, HEAD), lambda h, i: (0, H + h)),
         pl.BlockSpec((S, HEAD), lambda h, i: (0, 2 * H + h)),
         pl.BlockSpec((None, TQ, 1), lambda h, i: (h, i, 0)), blk],
        [blk, full, full], [shp, shp, shp], [], (zb, zb, zb, ltot, dc), ("arbitrary", "arbitrary"))


def _rg_gates(u, ga, gx, lam):
    r = _sigmoid(ga)
    i = _sigmoid(gx)
    log_a = RG_C * r * _neg_softplus(-lam)
    a = jnp.exp(log_a)
    b = jnp.sqrt(-_expm1(2.0 * log_a)) * (i * u)
    return a, b


def _heads_matmul(u, w_ref, transpose):
    outs = []
    for hh in range(w_ref.shape[0]):
        uh = u[:, hh * HEAD:(hh + 1) * HEAD].astype(BF16)
        wv = w_ref[hh].astype(BF16)
        outs.append(_dot_nt(uh, wv) if transpose else _dot(uh, wv))
    return jnp.concatenate(outs, axis=1)


def _conv(buf, cw_ref, cb_ref, tb):
    u = cb_ref[...] + cw_ref[pl.ds(CONV_WIDTH - 1, 1), :] * buf[pl.ds(8, tb), :]
    for k in range(1, CONV_WIDTH):
        u = u + cw_ref[pl.ds(CONV_WIDTH - 1 - k, 1), :] * buf[pl.ds(8 - k, tb), :]
    return u


def _rnn_specs(tb, C, nb, col_gate, rev):
    blk_of = (lambda i: nb - 1 - i) if rev else (lambda i: i)
    blk = lambda col: pl.BlockSpec((tb, C), lambda i: (blk_of(i), col))
    halo = lambda col: pl.BlockSpec((8, C), lambda i: (jnp.maximum(blk_of(i) * (tb // 8) - 1, 0), col))
    return blk, halo


def rnn_fwd(z, cw, cb, wa, ba, wx, bx, lam):
    S = z.shape[0]
    C = cb.shape[-1]
    tb = _tile(S, 512)
    nb = S // tb

    def body(xg_ref, xr_ref, halo_ref, cw_ref, cb_ref, wa_ref, ba_ref, wx_ref, bx_ref, lam_ref,
             out_ref, hs_ref, buf, a_s, b_s, h_c):
        i = pl.program_id(0)

        @pl.when(i == 0)
        def _():
            h_c[...] = jnp.zeros_like(h_c)

        buf[pl.ds(0, 8), :] = jnp.where(i == 0, 0.0, halo_ref[...])
        buf[pl.ds(8, tb), :] = xr_ref[...]
        u = _conv(buf, cw_ref, cb_ref, tb)
        ga = _heads_matmul(u, wa_ref, False) + ba_ref[...]
        gx = _heads_matmul(u, wx_ref, False) + bx_ref[...]
        a, b = _rg_gates(u, ga, gx, lam_ref[...])
        a_s[...] = a
        b_s[...] = b

        def step(t, h):
            h = a_s[pl.ds(t, 1), :] * h + b_s[pl.ds(t, 1), :]
            hs_ref[pl.ds(t, 1), :] = h
            return h

        h_c[...] = lax.fori_loop(0, tb, step, h_c[...], unroll=8)
        out_ref[...] = (_gelu(xg_ref[...]) * hs_ref[...]).astype(BF16)

    blk, halo = _rnn_specs(tb, C, nb, 6, False)
    full = lambda a: pl.BlockSpec(a.shape, lambda i: (0,) * a.ndim)
    out = pl.BlockSpec((tb, C), lambda i: (i, 0))
    return pl.pallas_call(
        body, name="rnn_fwd", grid=(nb,),
        in_specs=[blk(6), blk(7), halo(7)] + [full(a) for a in (cw, cb, wa, ba, wx, bx, lam)],
        out_specs=[out, out],
        out_shape=[jax.ShapeDtypeStruct((S, C), BF16), jax.ShapeDtypeStruct((S, C), F32)],
        scratch_shapes=[pltpu.VMEM((tb + 8, C), F32), pltpu.VMEM((tb, C), F32), pltpu.VMEM((tb, C), F32),
                        pltpu.VMEM((1, C), F32)],
        compiler_params=_params("arbitrary"))(z, z, z, cw, cb, wa, ba, wx, bx, lam)


def rnn_bwd(z, hs, dc, cw, cb, wa, ba, wx, bx, lam):
    S = z.shape[0]
    C = cb.shape[-1]
    tb = _tile(S, 512)
    nb = S // tb

    def body(xg_ref, xr_ref, halo_ref, hs_ref, hs_halo_ref, do_ref,
             cw_ref, cb_ref, wa_ref, ba_ref, wx_ref, bx_ref, lam_ref,
             dxg_ref, dxr_ref, dcw_ref, dcb_ref, dwa_ref, dba_ref, dwx_ref, dbx_ref, dlam_ref,
             buf, hbuf, dubuf, a_s, dh_s, carry):
        i = pl.program_id(0)
        first = i == nb - 1

        @pl.when(i == 0)
        def _():
            carry[...] = jnp.zeros_like(carry)
            dubuf[pl.ds(tb, 8), :] = jnp.zeros((8, C), F32)
            for r in (dcw_ref, dcb_ref, dwa_ref, dba_ref, dwx_ref, dbx_ref, dlam_ref):
                r[...] = jnp.zeros_like(r)

        buf[pl.ds(0, 8), :] = jnp.where(first, 0.0, halo_ref[...])
        buf[pl.ds(8, tb), :] = xr_ref[...]
        u = _conv(buf, cw_ref, cb_ref, tb)
        ga = _heads_matmul(u, wa_ref, False) + ba_ref[...]
        gx = _heads_matmul(u, wx_ref, False) + bx_ref[...]
        (a, _), gates_vjp = jax.vjp(_rg_gates, u, ga, gx, lam_ref[...])
        gel, gelu_vjp = jax.vjp(_gelu, xg_ref[...])
        hs = hs_ref[...]
        do = do_ref[...]
        dxg_ref[...] = gelu_vjp(do * hs)[0].astype(BF16)
        dh_s[...] = do * gel
        a_s[...] = a

        def step(n, c):
            t = tb - 1 - n
            acc = dh_s[pl.ds(t, 1), :] + c
            dh_s[pl.ds(t, 1), :] = acc
            return a_s[pl.ds(t, 1), :] * acc

        carry[...] = lax.fori_loop(0, tb, step, carry[...], unroll=8)
        db = dh_s[...]
        hbuf[pl.ds(0, 8), :] = jnp.where(first, 0.0, hs_halo_ref[...])
        hbuf[pl.ds(8, tb), :] = hs
        da = db * hbuf[pl.ds(7, tb), :]
        du, dga, dgx, dlam = gates_vjp((da, db))
        du = du + _heads_matmul(dga, wa_ref, True) + _heads_matmul(dgx, wx_ref, True)
        dlam_ref[...] += dlam
        dba_ref[...] += jnp.sum(dga, axis=0, keepdims=True)
        dbx_ref[...] += jnp.sum(dgx, axis=0, keepdims=True)
        ub = u.astype(BF16)
        for hh in range(wa_ref.shape[0]):
            cols = slice(hh * HEAD, (hh + 1) * HEAD)
            dwa_ref[hh] += _dot_tn(ub[:, cols], dga[:, cols].astype(BF16))
            dwx_ref[hh] += _dot_tn(ub[:, cols], dgx[:, cols].astype(BF16))
        dcb_ref[...] += jnp.sum(du, axis=0, keepdims=True)
        dubuf[pl.ds(0, tb), :] = du
        dxr = cw_ref[pl.ds(CONV_WIDTH - 1, 1), :] * du
        dcw_ref[pl.ds(CONV_WIDTH - 1, 1), :] += jnp.sum(du * buf[pl.ds(8, tb), :], axis=0, keepdims=True)
        for k in range(1, CONV_WIDTH):
            dxr = dxr + cw_ref[pl.ds(CONV_WIDTH - 1 - k, 1), :] * dubuf[pl.ds(k, tb), :]
            dcw_ref[pl.ds(CONV_WIDTH - 1 - k, 1), :] += jnp.sum(du * buf[pl.ds(8 - k, tb), :], axis=0,
                                                                keepdims=True)
        dxr_ref[...] = dxr.astype(BF16)
        dubuf[pl.ds(tb, 8), :] = du[0:8, :]

    blk, halo = _rnn_specs(tb, C, nb, 6, True)
    full = lambda a: pl.BlockSpec(a.shape, lambda i: (0,) * a.ndim)
    out = pl.BlockSpec((tb, C), lambda i: (nb - 1 - i, 0))
    params = (cw, cb, wa, ba, wx, bx, lam)
    return pl.pallas_call(
        body, name="rnn_bwd", grid=(nb,),
        in_specs=[blk(6), blk(7), halo(7), blk(0), halo(0), blk(2)] + [full(a) for a in params],
        out_specs=[out, out] + [full(a) for a in params],
        out_shape=[jax.ShapeDtypeStruct((S, C), BF16)] * 2 + [jax.ShapeDtypeStruct(a.shape, F32) for a in params],
        scratch_shapes=[pltpu.VMEM((tb + 8, C), F32), pltpu.VMEM((tb + 8, C), F32), pltpu.VMEM((tb + 8, C), F32),
                        pltpu.VMEM((tb, C), F32), pltpu.VMEM((tb, C), F32), pltpu.VMEM((1, C), F32)],
        compiler_params=_params("arbitrary"))(z, z, z, hs, hs, dc, *params)


def _pool_stats(buf, x, t0, tb, n_groups):
    t = t0 + lax.broadcasted_iota(jnp.int32, (tb, 1), 0)
    ds, cnts = [], []
    for g in range(n_groups):
        win = POOL_WINDOWS[g]
        cols = slice(g * HEAD, (g + 1) * HEAD)
        s = buf[pl.ds(16, tb), cols]
        for k in range(1, win):
            s = s + buf[pl.ds(16 - k, tb), cols]
        cnt = jnp.minimum(t + 1, win).astype(F32)
        ds.append(s / cnt - x[:, cols])
        cnts.append(cnt)
    return ds, cnts


def _pool_specs(tb, C, nb, col, rev):
    blk_of = (lambda i: nb - 1 - i) if rev else (lambda i: i)
    blk = pl.BlockSpec((tb, C), lambda i: (blk_of(i), col))
    halo = pl.BlockSpec((16, C), lambda i: (jnp.maximum(blk_of(i) * (tb // 16) - 1, 0), col))
    return blk, halo


def pool_fwd(z, pw, ps):
    S = z.shape[0]
    C = ps.shape[-1]
    G = pw.shape[0]
    tb = _tile(S, 512)
    nb = S // tb

    def body(x_ref, halo_ref, pw_ref, ps_ref, out_ref, buf):
        i = pl.program_id(0)
        buf[pl.ds(0, 16), :] = jnp.where(i == 0, 0.0, halo_ref[...])
        x = x_ref[...]
        buf[pl.ds(16, tb), :] = x
        ds, _ = _pool_stats(buf, x, i * tb, tb, G)
        y = jnp.concatenate([_dot(ds[g].astype(BF16), pw_ref[g].astype(BF16)) for g in range(G)], axis=1)
        out_ref[...] = (y * ps_ref[...]).astype(BF16)

    blk, halo = _pool_specs(tb, C, nb, 8, False)
    full = lambda a: pl.BlockSpec(a.shape, lambda i: (0,) * a.ndim)
    return pl.pallas_call(
        body, name="pool_fwd", grid=(nb,), in_specs=[blk, halo, full(pw), full(ps)],
        out_specs=pl.BlockSpec((tb, C), lambda i: (i, 0)), out_shape=jax.ShapeDtypeStruct((S, C), BF16),
        scratch_shapes=[pltpu.VMEM((tb + 16, C), F32)], compiler_params=_params("arbitrary"))(z, z, pw, ps)


def pool_bwd(z, dc, pw, ps):
    S = z.shape[0]
    C = ps.shape[-1]
    G = pw.shape[0]
    tb = _tile(S, 512)
    nb = S // tb

    def body(x_ref, halo_ref, dy_ref, pw_ref, ps_ref, dx_ref, dpw_ref, dps_ref, buf, ebuf):
        i = pl.program_id(0)
        ib = nb - 1 - i

        @pl.when(i == 0)
        def _():
            ebuf[pl.ds(tb, 16), :] = jnp.zeros((16, C), F32)
            dpw_ref[...] = jnp.zeros_like(dpw_ref)
            dps_ref[...] = jnp.zeros_like(dps_ref)

        buf[pl.ds(0, 16), :] = jnp.where(ib == 0, 0.0, halo_ref[...])
        x = x_ref[...]
        buf[pl.ds(16, tb), :] = x
        ds, cnts = _pool_stats(buf, x, ib * tb, tb, G)
        dy = dy_ref[...]
        dyp = dy * ps_ref[...]
        ypre, dds, es = [], [], []
        for g in range(G):
            cols = slice(g * HEAD, (g + 1) * HEAD)
            db = ds[g].astype(BF16)
            wv = pw_ref[g].astype(BF16)
            dypb = dyp[:, cols].astype(BF16)
            ypre.append(_dot(db, wv))
            dd = _dot_nt(dypb, wv)
            dpw_ref[g] += _dot_tn(db, dypb)
            dds.append(dd)
            es.append(dd / cnts[g])
        dps_ref[...] += jnp.sum(dy * jnp.concatenate(ypre, axis=1), axis=0, keepdims=True)
        e = jnp.concatenate(es, axis=1)
        ebuf[pl.ds(0, tb), :] = e
        dxs = []
        for g in range(G):
            cols = slice(g * HEAD, (g + 1) * HEAD)
            s = es[g]
            for k in range(1, POOL_WINDOWS[g]):
                s = s + ebuf[pl.ds(k, tb), cols]
            dxs.append(s - dds[g])
        dx_ref[...] = jnp.concatenate(dxs, axis=1).astype(BF16)
        ebuf[pl.ds(tb, 16), :] = e[0:16, :]

    blk, halo = _pool_specs(tb, C, nb, 8, True)
    dyb, _ = _pool_specs(tb, C, nb, 3, True)
    full = lambda a: pl.BlockSpec(a.shape, lambda i: (0,) * a.ndim)
    return pl.pallas_call(
        body, name="pool_bwd", grid=(nb,), in_specs=[blk, halo, dyb, full(pw), full(ps)],
        out_specs=[pl.BlockSpec((tb, C), lambda i: (nb - 1 - i, 0)), full(pw), full(ps)],
        out_shape=[jax.ShapeDtypeStruct((S, C), BF16), jax.ShapeDtypeStruct(pw.shape, F32),
                   jax.ShapeDtypeStruct(ps.shape, F32)],
        scratch_shapes=[pltpu.VMEM((tb + 16, C), F32), pltpu.VMEM((tb + 16, C), F32)],
        compiler_params=_params("arbitrary"))(z, z, dc, pw, ps)


def adamw(w, g, m, v):
    R, N = w.shape
    tr = _tile(R, 256)

    def body(w_ref, g_ref, m_ref, v_ref, d_ref, m2_ref, v2_ref):
        gv = g_ref[...]
        m2 = ADAM_B1 * m_ref[...] + (1.0 - ADAM_B1) * gv
        v2 = ADAM_B2 * v_ref[...] + (1.0 - ADAM_B2) * (gv * gv)
        m_hat = m2 / (1.0 - ADAM_B1 ** ADAM_STEP)
        v_hat = v2 / (1.0 - ADAM_B2 ** ADAM_STEP)
        d_ref[...] = -ADAM_LR * (m_hat / (jnp.sqrt(v_hat) + ADAM_EPS) + ADAM_WD * w_ref[...])
        m2_ref[...] = m2
        v2_ref[...] = v2

    blk = pl.BlockSpec((tr, N), lambda i: (i, 0))
    shp = jax.ShapeDtypeStruct((R, N), F32)
    return pl.pallas_call(body, name="adamw", grid=(R // tr,), in_specs=[blk] * 4, out_specs=[blk] * 3,
                          out_shape=[shp] * 3, compiler_params=_params("arbitrary"))(w, g, m, v)


def own_slot(w, dtype, me_arr):
    L, K, N = w.shape
    tk = _tile(K, 512)

    def body(me_ref, w_ref, o_ref):
        o_ref[...] = w_ref[...].astype(dtype)

    return pl.pallas_call(
        body, name="own_slot",
        grid_spec=pltpu.PrefetchScalarGridSpec(
            num_scalar_prefetch=1, grid=(L, K // tk),
            in_specs=[pl.BlockSpec((None, tk, N), lambda l, i, me: (l, i, 0))],
            out_specs=pl.BlockSpec((None, None, tk, N), lambda l, i, me: (me[0], l, i, 0))),
        out_shape=jax.ShapeDtypeStruct((N_CHIP, L, K, N), dtype),
        compiler_params=_params("arbitrary", "arbitrary"))(me_arr, w)


def swap_half_rows(g):
    K = g.shape[1]
    Kh = K // 2

    def body(g_ref, out_ref, send_sem, recv_sem):
        x, y, c, _ = _place()
        cp = pltpu.make_async_remote_copy(
            src_ref=g_ref.at[:, pl.ds((1 - c) * Kh, Kh), :], dst_ref=out_ref, send_sem=send_sem,
            recv_sem=recv_sem, device_id=(x, y, 1 - c), device_id_type=MESH)
        cp.start()
        cp.wait()

    return pl.pallas_call(
        body, name="swap_half_rows", in_specs=[HBM], out_specs=HBM,
        out_shape=jax.ShapeDtypeStruct((N_CHIP, Kh, g.shape[2]), g.dtype),
        scratch_shapes=[pltpu.SemaphoreType.DMA, pltpu.SemaphoreType.DMA])(g)


def chip_sum(g, got, c_arr):
    _, K, N = g.shape
    Kh = K // 2
    tr = _tile(Kh, 352 if Kh % 352 == 0 else 256)

    def body(c_ref, g_ref, got_ref, o_ref):
        o_ref[...] = (g_ref[...] + got_ref[...]).astype(BF16)

    nt = Kh // tr
    return pl.pallas_call(
        body, name="chip_sum",
        grid_spec=pltpu.PrefetchScalarGridSpec(
            num_scalar_prefetch=1, grid=(N_CHIP, nt),
            in_specs=[pl.BlockSpec((None, tr, N), lambda j, i, c: (j, c[0] * nt + i, 0)),
                      pl.BlockSpec((None, tr, N), lambda j, i, c: (j, i, 0))],
            out_specs=pl.BlockSpec((None, tr, N), lambda j, i, c: (j, i, 0))),
        out_shape=jax.ShapeDtypeStruct((N_CHIP, Kh, N), BF16),
        compiler_params=_params("arbitrary", "arbitrary"))(c_arr, g, got)


def total_sum(cs, got, me_arr, c_arr, l, into):
    _, Kh, N = cs.shape
    tr = _tile(Kh, 352 if Kh % 352 == 0 else 256)
    nt = Kh // tr

    def body(me_ref, c_ref, cs_ref, got_ref, *rest):
        o_ref = rest[-1]
        o_ref[...] = ((cs_ref[...].astype(F32) + got_ref[0].astype(F32)) + got_ref[1].astype(F32)) + \
            got_ref[2].astype(F32)

    ins = [pl.BlockSpec((None, tr, N), lambda i, me, c: (me[0], i, 0)),
           pl.BlockSpec((3, tr, N), lambda i, me, c: (0, i, 0))]
    args = [me_arr, c_arr, cs, got]
    alias = {}
    if into is not None:
        ins.append(pl.BlockSpec(memory_space=pl.ANY))
        args.append(into)
        alias = {4: 0}
    return pl.pallas_call(
        body, name="total_sum",
        grid_spec=pltpu.PrefetchScalarGridSpec(
            num_scalar_prefetch=2, grid=(nt,), in_specs=ins,
            out_specs=pl.BlockSpec((None, tr, N), lambda i, me, c: (l, c[0] * nt + i, 0))),
        out_shape=jax.ShapeDtypeStruct((2, 2 * Kh, N), F32), input_output_aliases=alias,
        compiler_params=_params("arbitrary"))(*args)


def join_halves(buf):
    Kh = buf.shape[1] // 2

    def body(in_ref, out_ref, send_sem, recv_sem):
        x, y, c, _ = _place()
        mine = out_ref.at[:, pl.ds(c * Kh, Kh), :]
        cp = pltpu.make_async_remote_copy(
            src_ref=mine, dst_ref=mine, send_sem=send_sem, recv_sem=recv_sem,
            device_id=(x, y, 1 - c), device_id_type=MESH)
        cp.start()
        cp.wait()

    return pl.pallas_call(
        body, name="join_halves", in_specs=[HBM], out_specs=HBM,
        out_shape=jax.ShapeDtypeStruct(buf.shape, F32), input_output_aliases={0: 0},
        scratch_shapes=[pltpu.SemaphoreType.DMA, pltpu.SemaphoreType.DMA])(buf)


def allreduce_small(p):
    R = p.shape[0]

    def body(p_ref, out_ref, all_ref, send_sems, recv_sems, local_sem):
        x, y, c, others = _place()
        me, sibling = (x, y, c), (x, y, 1 - c)

        def rows(px, py, pc):
            return all_ref.at[pl.ds((4 * px + 2 * py + pc) * R, R), :]

        def copy(k, block, to, src=None):
            return pltpu.make_async_remote_copy(
                src_ref=rows(*block) if src is None else src, dst_ref=rows(*block), send_sem=send_sems.at[k],
                recv_sem=recv_sems.at[k], device_id=to, device_id_type=MESH)

        mine = pltpu.make_async_copy(p_ref, rows(*me), local_sem)
        mine.start()
        first = [copy(0, me, sibling, src=p_ref)]
        first += [copy(1 + j, me, (*chip, c), src=p_ref) for j, chip in enumerate(others)]
        for cp in first:
            cp.start()
        passed = [copy(4 + j, (*chip, c), sibling) for j, chip in enumerate(others)]
        for j, chip in enumerate(others):
            copy(1 + j, (*chip, c), me).wait_recv()
            passed[j].start()
        copy(0, sibling, me).wait_recv()
        for j, chip in enumerate(others):
            copy(4 + j, (*chip, 1 - c), me).wait_recv()
        for cp in first + passed:
            cp.wait_send()
        mine.wait()
        total = all_ref[pl.ds(0, R), :]
        for d in range(1, N_DEV):
            total = total + all_ref[pl.ds(d * R, R), :]
        out_ref[...] = total

    vmem = pl.BlockSpec(memory_space=pltpu.VMEM)
    return pl.pallas_call(
        body, name="allreduce_small", in_specs=[vmem], out_specs=vmem,
        out_shape=jax.ShapeDtypeStruct((R, 128), F32),
        scratch_shapes=[pltpu.VMEM((N_DEV * R, 128), F32), pltpu.SemaphoreType.DMA((7,)),
                        pltpu.SemaphoreType.DMA((7,)), pltpu.SemaphoreType.DMA],
        compiler_params=pltpu.CompilerParams(vmem_limit_bytes=VMEM_LIMIT_V7X))(p)


def _ffn_fwd(x, g, Wg, names, l, slots, late):
    h = rmsnorm_fwd(x, g)
    (a, b, y), got = ffn_up(h, Wg[names[0]], Wg[names[1]], l, [_gather_job(slots[n]) for n in late])
    for n, o in zip(late, got):
        Wg[n] = o[0]
    return residual_matmul(x, y, Wg[names[2]], l, 0.5, True), (x, h, a, b, y)


def _ffn_bwd(dxo, saved, g, wg, wu, wd, l):
    x, h, a, b, y = saved
    D = x.shape[1]
    F = wg.shape[-1]
    da, db = ffn_bwd_hidden(dxo, wd, a, b, l)
    g_wd = weight_grad(y, dxo, F, D, True, False, b_scale=0.5)
    g_wg = weight_grad(h, da, D, F, False, True)
    g_wu = weight_grad(h, db, D, F, False, True)
    dx, dg = norm_bwd_matmul(x, g, dxo, [(da, wg), (db, wu)], l, True)
    return dx, dg, g_wg, g_wu, g_wd


def kernel(x, norm_ffn1, ffn1_gate, ffn1_up, ffn1_down, norm_mix, w_in, conv_w, conv_b, rg_w_a, rg_b_a, rg_w_x, rg_b_x, rg_lambda, pool_w, pool_scale, w_out, norm_ffn2, ffn2_gate, ffn2_up, ffn2_down, norm_final, loss_target, m_norm_ffn1, m_ffn1_gate, m_ffn1_up, m_ffn1_down, m_norm_mix, m_w_in, m_conv_w, m_conv_b, m_rg_w_a, m_rg_b_a, m_rg_w_x, m_rg_b_x, m_rg_lambda, m_pool_w, m_pool_scale, m_w_out, m_norm_ffn2, m_ffn2_gate, m_ffn2_up, m_ffn2_down, m_norm_final, v_norm_ffn1, v_ffn1_gate, v_ffn1_up, v_ffn1_down, v_norm_mix, v_w_in, v_conv_w, v_conv_b, v_rg_w_a, v_rg_b_a, v_rg_w_x, v_rg_b_x, v_rg_lambda, v_pool_w, v_pool_scale, v_w_out, v_norm_ffn2, v_ffn2_gate, v_ffn2_up, v_ffn2_down, v_norm_final):
    given = dict(locals())
    W = {n: given[n] for n in WEIGHTS}
    M = {n: given["m_" + n] for n in WEIGHTS}
    V = {n: given["v_" + n] for n in WEIGHTS}
    depth = norm_ffn1.shape[0]
    S, D = x.shape[1], x.shape[2]
    xs = x.reshape(S, D)
    target = loss_target.reshape(S, D)
    C = conv_b.shape[-1]
    n_heads = (D // 2) // HEAD
    cx, cy, cc = lax.axis_index("x"), lax.axis_index("y"), lax.axis_index("c")
    me = 2 * cx + cy
    me_arr = jnp.reshape(me, (1,)).astype(jnp.int32)
    c_arr = jnp.reshape(cc, (1,)).astype(jnp.int32)

    slots = {n: own_slot(W[n], BF16, me_arr) for n in BIG}
    Wg = {n: _run_jobs([_gather_job(slots[n])], "gather_weight")[0][0] for n in ('ffn1_gate', 'ffn1_up')}
    conv_all = _run_jobs([_gather_job(own_slot(conv_w, F32, me_arr))], "gather_weight")[0][0]
    conv_full = jnp.transpose(conv_all, (1, 2, 0, 3)).reshape(depth, CONV_WIDTH, C)

    row = lambda a, l: a[l].reshape(1, -1)

    saved = []
    cur = xs
    for l in range(depth):
        cur, s1 = _ffn_fwd(cur, row(norm_ffn1, l), Wg, ('ffn1_gate', 'ffn1_up', 'ffn1_down'), l, slots,
                           ['ffn1_down', 'w_in'] if l == 0 else [])
        x1 = cur
        hm = rmsnorm_fwd(x1, row(norm_mix, l))
        z, zb = mix_in(hm, Wg['w_in'], l)
        late = ['w_out', 'ffn2_gate', 'ffn2_up', 'ffn2_down'] if l == 0 else []
        (att, ltot), got = attn_fwd(zb, n_heads, [_gather_job(slots[n]) for n in late])
        for n, o in zip(late, got):
            Wg[n] = o[0]
        rnn_p = (conv_full[l], row(conv_b, l), rg_w_a[l], row(rg_b_a, l), rg_w_x[l], row(rg_b_x, l),
                 row(rg_lambda, l))
        rnn, hs = rnn_fwd(z, *rnn_p)
        pool = pool_fwd(z, pool_w[l], row(pool_scale, l))
        cat = jnp.concatenate([att, rnn, pool], axis=1)
        cur = residual_matmul(x1, cat, Wg['w_out'], l, 1.0, False)
        cur, s2 = _ffn_fwd(cur, row(norm_ffn2, l), Wg, ('ffn2_gate', 'ffn2_up', 'ffn2_down'), l, slots, [])
        saved.append((s1, (x1, hm, z, zb, ltot, hs, cat, rnn_p), s2))

    loss_cols, dx, g_norm_final = final_loss(cur, norm_final.reshape(1, D), target)

    G = {n: [None] * depth for n in WEIGHTS if n != 'norm_final'}
    pending, landed = [], {}

    def reduce_start(names, l, grads):
        for n, g in zip(names, grads):
            pending.append((n, l, chip_sum(g, swap_half_rows(g), c_arr)))

    def scatter_jobs():
        jobs = [_scatter_job(cs) for _, _, cs in pending]
        return jobs

    def scatter_done(got):
        for (n, l, cs), o in zip(pending, got):
            landed[n, l] = (cs, o[0])
        pending.clear()

    for l in reversed(range(depth)):
        s1, (x1, hm, z, zb, ltot, hs, cat, rnn_p), s2 = saved[l]
        dx, G['norm_ffn2'][l], g_wg, g_wu, g_wd = _ffn_bwd(dx, s2, row(norm_ffn2, l), Wg['ffn2_gate'],
                                                          Wg['ffn2_up'], Wg['ffn2_down'], l)
        reduce_start(('ffn2_gate', 'ffn2_up', 'ffn2_down'), l, (g_wg, g_wu, g_wd))
        dc = mix_out_bwd(dx, Wg['w_out'], l)
        reduce_start(('w_out',), l, (weight_grad(cat, dx, C, D, False, False),))
        (dq, dk, dv), got = attn_bwd(zb, ltot, dc, n_heads, scatter_jobs())
        scatter_done(got)
        (dxg, dxr, G['conv_w'][l], G['conv_b'][l], G['rg_w_a'][l], G['rg_b_a'][l], G['rg_w_x'][l],
         G['rg_b_x'][l], G['rg_lambda'][l]) = rnn_bwd(z, hs, dc, *rnn_p)
        dxp, G['pool_w'][l], G['pool_scale'][l] = pool_bwd(z, dc, pool_w[l], row(pool_scale, l))
        dz = jnp.concatenate([dq.astype(BF16), dk.astype(BF16), dv.astype(BF16), dxg, dxr, dxp], axis=1)
        Nb = dz.shape[1] // N_CHIP
        reduce_start(('w_in',), l, (weight_grad(hm, dz, D, Nb, False, False),))
        dx, G['norm_mix'][l] = norm_bwd_matmul(x1, row(norm_mix, l), dx, [(dz, Wg['w_in'])], l, False)
        dx, G['norm_ffn1'][l], g_wg, g_wu, g_wd = _ffn_bwd(dx, s1, row(norm_ffn1, l), Wg['ffn1_gate'],
                                                          Wg['ffn1_up'], Wg['ffn1_down'], l)
        reduce_start(('ffn1_gate', 'ffn1_up', 'ffn1_down'), l, (g_wg, g_wu, g_wd))
    grad_x = dx.reshape(x.shape)
    scatter_done(_run_jobs(scatter_jobs(), "scatter_chip_sums"))

    outs = {}
    for n in BIG:
        halves = None
        for l in range(depth):
            cs, got = landed[n, l]
            halves = total_sum(cs, got, me_arr, c_arr, l, halves)
        g_red = join_halves(halves)
        two_d = (-1, g_red.shape[-1])
        d, m2, v2 = adamw(W[n].reshape(two_d), g_red.reshape(two_d), M[n].reshape(two_d), V[n].reshape(two_d))
        outs[n] = (g_red, d.reshape(W[n].shape), m2.reshape(W[n].shape), v2.reshape(W[n].shape))

    loss_part = 0.5 * jnp.sum(loss_cols) / D
    parts = [jnp.stack(G[n]).reshape(-1) for n in SMALL if n != 'norm_final']
    parts += [g_norm_final.reshape(-1), jnp.reshape(loss_part, (1,))]
    sizes = [p.shape[0] for p in parts]
    total = sum(sizes)
    R = -(-total // 1024) * 8
    packed = jnp.concatenate(parts + [jnp.zeros((R * 128 - total,), F32)]).reshape(R, 128)
    red = allreduce_small(packed).reshape(-1)
    offs = [sum(sizes[:i]) for i in range(len(sizes))]
    small_names = [n for n in SMALL if n != 'norm_final'] + ['norm_final']
    small_g = {}
    for n, o, sz in zip(small_names, offs, sizes):
        if n == 'conv_w':
            full = red[o:o + sz].reshape(depth, CONV_WIDTH, C)
            small_g[n] = lax.dynamic_slice_in_dim(full, me * (C // N_CHIP), C // N_CHIP, axis=2)
        else:
            small_g[n] = red[o:o + sz].reshape(W[n].shape)
    loss = red[offs[-1]]

    def pack(d):
        flat = jnp.concatenate([d[n].reshape(-1) for n in small_names])
        rows = -(-flat.shape[0] // 1024) * 8
        return jnp.concatenate([flat, jnp.ones((rows * 128 - flat.shape[0],), F32)]).reshape(rows, 128)

    d_s, m_s, v_s = adamw(pack(W), pack(small_g), pack(M), pack(V))
    o = 0
    for n in small_names:
        sz = W[n].size
        outs[n] = (small_g[n],) + tuple(a.reshape(-1)[o:o + sz].reshape(W[n].shape) for a in (d_s, m_s, v_s))
        o += sz

    return (loss, grad_x, *[outs[n][0] for n in WEIGHTS], *[outs[n][1] for n in WEIGHTS],
            *[outs[n][2] for n in WEIGHTS], *[outs[n][3] for n in WEIGHTS])
```

```python
import functools

import jax
import jax.numpy as jnp
from jax import lax
from jax.experimental import pallas as pl
from jax.experimental.pallas import tpu as pltpu

F32 = jnp.float32
BF16 = jnp.bfloat16
MESH = pl.DeviceIdType.MESH

HEAD = 128
ATTN_TQ, ATTN_TK = 1024, 256
ATTN_TQ_FWD = 2048
POOL_WINDOWS = (2, 4, 8, 16)
CONV_WIDTH = 4
RG_C = 8.0
NORM_EPS = 1e-6
N_CHIP = 4
N_DEV = 8
ADAM_LR, ADAM_B1, ADAM_B2, ADAM_EPS, ADAM_WD, ADAM_STEP = 0.001, 0.9, 0.999, 1e-08, 0.01, 10
VMEM_LIMIT_V7X = 56 * 2 ** 20

WEIGHTS = ['norm_ffn1', 'ffn1_gate', 'ffn1_up', 'ffn1_down', 'norm_mix', 'w_in', 'conv_w', 'conv_b', 'rg_w_a',
           'rg_b_a', 'rg_w_x', 'rg_b_x', 'rg_lambda', 'pool_w', 'pool_scale', 'w_out', 'norm_ffn2', 'ffn2_gate',
           'ffn2_up', 'ffn2_down', 'norm_final']
BIG = ['ffn1_gate', 'ffn1_up', 'ffn1_down', 'w_in', 'w_out', 'ffn2_gate', 'ffn2_up', 'ffn2_down']
SMALL = [n for n in WEIGHTS if n not in BIG]


def _params(*sem):
    return pltpu.CompilerParams(dimension_semantics=sem, vmem_limit_bytes=VMEM_LIMIT_V7X)


def _tile(n, pref):
    t = min(n, pref)
    while n % t:
        t //= 2
    return t


def _dot(a, b):
    return jnp.dot(a, b, preferred_element_type=F32)


def _dot_nt(a, b):
    return lax.dot_general(a, b, (((1,), (1,)), ((), ())), preferred_element_type=F32)


def _dot_tn(a, b):
    return lax.dot_general(a, b, (((0,), (0,)), ((), ())), preferred_element_type=F32)


def _sigmoid(x):
    return 1.0 / (1.0 + jnp.exp(-x))


def _gelu(x):
    return 0.5 * x * (1.0 + jnp.tanh(0.7978845608028654 * (x + 0.044715 * x * x * x)))


def _expm1(x):
    series = x * (1.0 + x * 0.5 * (1.0 + x * (1.0 / 3.0) * (1.0 + x * 0.25 * (1.0 + x * 0.2))))
    return jnp.where(jnp.abs(x) < 0.1, series, jnp.exp(x) - 1.0)


def _neg_softplus(z):
    return -(jnp.maximum(z, 0.0) + jnp.log(1.0 + jnp.exp(-jnp.abs(z))))


def _split_bf16(x):
    hi = x.astype(BF16)
    return hi, (x - hi.astype(F32)).astype(BF16)


def _wspec(k, n, l, jpos):
    return pl.BlockSpec((None, None, k, n), lambda *g: (g[jpos], l, 0, 0))


HBM = pl.BlockSpec(memory_space=pltpu.HBM)


def _place():
    x, y, c = lax.axis_index("x"), lax.axis_index("y"), lax.axis_index("c")
    others = [(1 - x, y), (x, 1 - y), (1 - x, 1 - y)]
    return x, y, c, others


def _scatter_job(cs):
    def copies(ins, outs, sems):
        x, y, c, others = _place()
        return [pltpu.make_async_remote_copy(
            src_ref=ins[0].at[2 * chip[0] + chip[1]], dst_ref=outs[0].at[k], send_sem=sems[0].at[k],
            recv_sem=sems[1].at[k], device_id=(*chip, c), device_id_type=MESH) for k, chip in enumerate(others)]

    def start(ins, outs, sems):
        for cp in copies(ins, outs, sems):
            cp.start()

    def finish(ins, outs, sems):
        for cp in copies(ins, outs, sems):
            cp.wait()

    return dict(ins=[cs], outs=[jax.ShapeDtypeStruct((3,) + cs.shape[1:], cs.dtype)], alias={},
                sems=[pltpu.SemaphoreType.DMA((3,)), pltpu.SemaphoreType.DMA((3,))], start=start, finish=finish)


def _swap_job(g):
    Kh = g.shape[1] // 2

    def copy(ins, outs, sems):
        x, y, c, _ = _place()
        return pltpu.make_async_remote_copy(
            src_ref=ins[0].at[:, pl.ds((1 - c) * Kh, Kh), :], dst_ref=outs[0], send_sem=sems[0], recv_sem=sems[1],
            device_id=(x, y, 1 - c), device_id_type=MESH)

    return dict(ins=[g], outs=[jax.ShapeDtypeStruct((N_CHIP, Kh, g.shape[2]), g.dtype)], alias={},
                sems=[pltpu.SemaphoreType.DMA, pltpu.SemaphoreType.DMA],
                start=lambda ins, outs, sems: copy(ins, outs, sems).start(),
                finish=lambda ins, outs, sems: copy(ins, outs, sems).wait())


def _gather_job(buf):
    def copy(outs, sems, k, chip, layer, to):
        dst = outs[0].at[2 * chip[0] + chip[1], layer]
        return pltpu.make_async_remote_copy(src_ref=dst, dst_ref=dst, send_sem=sems[0].at[k], recv_sem=sems[1].at[k],
                                            device_id=to, device_id_type=MESH)

    def start(ins, outs, sems):
        x, y, c, others = _place()
        for k, chip in enumerate(others):
            copy(outs, sems, k, (x, y), c, (*chip, c)).start()

    def finish(ins, outs, sems):
        x, y, c, others = _place()
        me, sibling = (x, y, c), (x, y, 1 - c)
        for k, chip in enumerate(others):
            copy(outs, sems, k, chip, c, me).wait_recv()
            copy(outs, sems, 3 + k, chip, c, sibling).start()
        for k, chip in enumerate(others):
            copy(outs, sems, 3 + k, chip, 1 - c, me).wait_recv()
        for k in range(6):
            copy(outs, sems, k, (x, y), c, me).wait_send()

    return dict(ins=[buf], outs=[jax.ShapeDtypeStruct(buf.shape, buf.dtype)], alias={0: 0},
                sems=[pltpu.SemaphoreType.DMA((6,)), pltpu.SemaphoreType.DMA((6,))], start=start, finish=finish)


def _carried_call(body, jobs, name, grid, in_specs, out_specs, out_shape, scratch_shapes, args, sem):
    n_in, n_out, n_scr = len(in_specs), len(out_specs), len(scratch_shapes)
    j_in = [a for j in jobs for a in j['ins']]
    j_out = [o for j in jobs for o in j['outs']]
    j_sem = [s for j in jobs for s in j['sems']]

    def wrapped(*refs):
        own_in, ji = refs[:n_in], list(refs[n_in:n_in + len(j_in)])
        rest = refs[n_in + len(j_in):]
        own_out, jo = rest[:n_out], list(rest[n_out:n_out + len(j_out)])
        rest = rest[n_out + len(j_out):]
        own_scr, js = rest[:n_scr], list(rest[n_scr:])
        per_job = []
        for j in jobs:
            per_job.append((j, [ji.pop(0) for _ in j['ins']], [jo.pop(0) for _ in j['outs']],
                            [js.pop(0) for _ in j['sems']]))
        ids = [pl.program_id(d) for d in range(len(grid))]
        first = functools.reduce(lambda a, b: a & b, [i == 0 for i in ids])
        last = functools.reduce(lambda a, b: a & b, [i == g - 1 for i, g in zip(ids, grid)])

        if jobs:
            @pl.when(first)
            def _():
                for j, a, o, s in per_job:
                    j['start'](a, o, s)

        body(*own_in, *own_out, *own_scr)

        if jobs:
            @pl.when(last)
            def _():
                for j, a, o, s in per_job:
                    j['finish'](a, o, s)

    alias, off_in, off_out = {}, n_in, n_out
    for j in jobs:
        for li, lo in j['alias'].items():
            alias[off_in + li] = off_out + lo
        off_in += len(j['ins'])
        off_out += len(j['outs'])
    res = pl.pallas_call(
        wrapped, name=name, grid=grid, in_specs=list(in_specs) + [HBM] * len(j_in),
        out_specs=list(out_specs) + [HBM] * len(j_out), out_shape=list(out_shape) + j_out,
        scratch_shapes=list(scratch_shapes) + j_sem, input_output_aliases=alias,
        compiler_params=_params(*sem))(*args, *j_in)
    own, rest = list(res[:n_out]), list(res[n_out:])
    return own, [[rest.pop(0) for _ in j['outs']] for j in jobs]


def _run_jobs(jobs, name):
    def body(*refs):
        ins, rest = list(refs[:n_in]), list(refs[n_in:])
        outs, sems = rest[:n_out], rest[n_out:]
        per_job = [(j, [ins.pop(0) for _ in j['ins']], [outs.pop(0) for _ in j['outs']],
                    [sems.pop(0) for _ in j['sems']]) for j in jobs]
        for j, a, o, s in per_job:
            j['start'](a, o, s)
        for j, a, o, s in per_job:
            j['finish'](a, o, s)

    j_in = [a for j in jobs for a in j['ins']]
    j_out = [o for j in jobs for o in j['outs']]
    n_in, n_out = len(j_in), len(j_out)
    alias, off_in, off_out = {}, 0, 0
    for j in jobs:
        for li, lo in j['alias'].items():
            alias[off_in + li] = off_out + lo
        off_in += len(j['ins'])
        off_out += len(j['outs'])
    res = list(pl.pallas_call(
        body, name=name, in_specs=[HBM] * n_in, out_specs=[HBM] * n_out, out_shape=j_out,
        scratch_shapes=[s for j in jobs for s in j['sems']], input_output_aliases=alias)(*j_in))
    return [[res.pop(0) for _ in j['outs']] for j in jobs]


def rmsnorm_fwd(x, g):
    S, D = x.shape
    tm = _tile(S, 512)

    def body(x_ref, g_ref, h_ref):
        xv = x_ref[...]
        rstd = lax.rsqrt(jnp.mean(xv * xv, axis=-1, keepdims=True) + NORM_EPS)
        h_ref[...] = (xv * rstd * g_ref[...]).astype(BF16)

    return pl.pallas_call(
        body, name="rmsnorm_fwd", grid=(S // tm,),
        in_specs=[pl.BlockSpec((tm, D), lambda i: (i, 0)), pl.BlockSpec((1, D), lambda i: (0, 0))],
        out_specs=pl.BlockSpec((tm, D), lambda i: (i, 0)),
        out_shape=jax.ShapeDtypeStruct((S, D), BF16), compiler_params=_params("arbitrary"))(x, g)


def ffn_up(h, wg, wu, l, jobs=()):
    S, D = h.shape
    F = wg.shape[-1]
    tm = _tile(S, 512)

    def body(h_ref, wg_ref, wu_ref, a_ref, b_ref, y_ref):
        hv = h_ref[...]
        a = _dot(hv, wg_ref[...])
        b = _dot(hv, wu_ref[...])
        a_ref[...] = a.astype(BF16)
        b_ref[...] = b.astype(BF16)
        y_ref[...] = (a * _sigmoid(a) * b).astype(BF16)

    out = pl.BlockSpec((None, tm, F), lambda j, i: (j, i, 0))
    shp = jax.ShapeDtypeStruct((N_CHIP, S, F), BF16)
    return _carried_call(
        body, jobs, "ffn_up", (N_CHIP, S // tm),
        [pl.BlockSpec((tm, D), lambda j, i: (i, 0)), _wspec(D, F, l, 0), _wspec(D, F, l, 0)],
        [out, out, out], [shp, shp, shp], [], (h, wg, wu), ("arbitrary", "arbitrary"))


def residual_matmul(x, a, w, l, scale, a_blocked):
    S, D = x.shape
    Kb = w.shape[2]
    tm = _tile(S, 512)

    def body(x_ref, a_ref, w_ref, o_ref, acc):
        j = pl.program_id(1)

        @pl.when(j == 0)
        def _():
            acc[...] = jnp.zeros_like(acc)

        acc[...] += _dot(a_ref[...], w_ref[...])

        @pl.when(j == N_CHIP - 1)
        def _():
            o_ref[...] = x_ref[...] + scale * acc[...]

    a_spec = (pl.BlockSpec((None, tm, Kb), lambda i, j: (j, i, 0)) if a_blocked
              else pl.BlockSpec((tm, Kb), lambda i, j: (i, j)))
    return pl.pallas_call(
        body, name="residual_matmul", grid=(S // tm, N_CHIP),
        in_specs=[pl.BlockSpec((tm, D), lambda i, j: (i, 0)), a_spec, _wspec(Kb, D, l, 1)],
        out_specs=pl.BlockSpec((tm, D), lambda i, j: (i, 0)),
        out_shape=jax.ShapeDtypeStruct((S, D), F32),
        scratch_shapes=[pltpu.VMEM((tm, D), F32)],
        compiler_params=_params("arbitrary", "arbitrary"))(x, a, w)


def norm_bwd_matmul(x, g, dxo, pairs, l, g_blocked, jobs=()):
    S, D = x.shape
    Nb = pairs[0][1].shape[-1]
    tm = _tile(S, 256)
    n_p = len(pairs)

    def body(*refs):
        x_ref, g_ref, dxo_ref = refs[:3]
        g_refs = refs[3:3 + n_p]
        w_refs = refs[3 + n_p:3 + 2 * n_p]
        dx_ref, dg_ref, acc = refs[3 + 2 * n_p:]
        i, j = pl.program_id(0), pl.program_id(1)

        @pl.when(j == 0)
        def _():
            acc[...] = jnp.zeros_like(acc)

        @pl.when((i == 0) & (j == 0))
        def _():
            dg_ref[...] = jnp.zeros_like(dg_ref)

        for g_r, w_r in zip(g_refs, w_refs):
            acc[...] += _dot_nt(g_r[...], w_r[...])

        @pl.when(j == N_CHIP - 1)
        def _():
            xv = x_ref[...]
            rstd = lax.rsqrt(jnp.mean(xv * xv, axis=-1, keepdims=True) + NORM_EPS)
            xhat = xv * rstd
            dh = acc[...]
            gd = dh * g_ref[...]
            dx_ref[...] = dxo_ref[...] + rstd * (gd - xhat * jnp.mean(gd * xhat, axis=-1, keepdims=True))
            dg_ref[...] += jnp.sum(dh * xhat, axis=0, keepdims=True)

    row = pl.BlockSpec((tm, D), lambda i, j: (i, 0))
    vec = pl.BlockSpec((1, D), lambda i, j: (0, 0))
    g_spec = (pl.BlockSpec((None, tm, Nb), lambda i, j: (j, i, 0)) if g_blocked
              else pl.BlockSpec((tm, Nb), lambda i, j: (i, j)))
    return _carried_call(
        body, jobs, "norm_bwd_matmul", (S // tm, N_CHIP),
        [row, vec, row] + [g_spec] * n_p + [_wspec(D, Nb, l, 1)] * n_p, [row, vec],
        [jax.ShapeDtypeStruct((S, D), F32), jax.ShapeDtypeStruct((1, D), F32)], [pltpu.VMEM((tm, D), F32)],
        (x, g, dxo, *[p[0] for p in pairs], *[p[1] for p in pairs]), ("arbitrary", "arbitrary"))


def ffn_bwd_hidden(dxo, wd, a, b, l):
    S, D = dxo.shape
    F = wd.shape[2]
    tm = _tile(S, 512)

    def body(dxo_ref, wd_ref, a_ref, b_ref, da_ref, db_ref):
        dy = 0.5 * _dot_nt(dxo_ref[...].astype(BF16), wd_ref[...])
        av = a_ref[...].astype(F32)
        bv = b_ref[...].astype(F32)
        s = _sigmoid(av)
        da_ref[...] = (dy * bv * (s * (1.0 + av * (1.0 - s)))).astype(BF16)
        db_ref[...] = (dy * (av * s)).astype(BF16)

    blk = pl.BlockSpec((None, tm, F), lambda j, i: (j, i, 0))
    shp = jax.ShapeDtypeStruct((N_CHIP, S, F), BF16)
    return pl.pallas_call(
        body, name="ffn_bwd_hidden", grid=(N_CHIP, S // tm),
        in_specs=[pl.BlockSpec((tm, D), lambda j, i: (i, 0)), _wspec(F, D, l, 0), blk, blk],
        out_specs=[blk, blk], out_shape=[shp, shp],
        compiler_params=_params("arbitrary", "arbitrary"))(dxo, wd, a, b)


def weight_grad(a, b, m, n, a_blocked, b_blocked, b_scale=1.0, jobs=()):
    S = a.shape[-2]
    ts = _tile(S, 1024)
    tm = m if m % 256 else _tile(m, 1024)
    tn = n if n % 256 else _tile(n, 1024)
    if m * n * 4 <= 6 * 2 ** 20:
        tm, tn = m, n

    def spec(arr, blocked, width, t, pos):
        if blocked:
            return pl.BlockSpec((None, ts, t), lambda j, mi, ni, si: (j, si, (mi, ni)[pos]))
        if arr.shape[-1] == width:
            return pl.BlockSpec((ts, t), lambda j, mi, ni, si: (si, (mi, ni)[pos]))
        nblk = width // t
        return pl.BlockSpec((ts, t), lambda j, mi, ni, si: (si, j * nblk + (mi, ni)[pos]))

    def body(a_ref, b_ref, o_ref):
        @pl.when(pl.program_id(3) == 0)
        def _():
            o_ref[...] = jnp.zeros_like(o_ref)

        bv = b_ref[...]
        if b_scale != 1.0:
            bv = b_scale * bv
        o_ref[...] += _dot_tn(a_ref[...], bv.astype(BF16))

    (out,), got = _carried_call(
        body, jobs, "weight_grad", (N_CHIP, m // tm, n // tn, S // ts),
        [spec(a, a_blocked, m, tm, 0), spec(b, b_blocked, n, tn, 1)],
        [pl.BlockSpec((None, tm, tn), lambda j, mi, ni, si: (j, mi, ni))],
        [jax.ShapeDtypeStruct((N_CHIP, m, n), F32)], [], (a, b),
        ("arbitrary", "arbitrary", "arbitrary", "arbitrary"))
    return out, got


def mix_in(h, w_in, l):
    S, D = h.shape
    Nb = w_in.shape[-1]
    tm = _tile(S, 512)

    def body(h_ref, w_ref, z_ref, zb_ref):
        z = _dot(h_ref[...], w_ref[...])
        z_ref[...] = z
        zb_ref[...] = z.astype(BF16)

    out = pl.BlockSpec((tm, Nb), lambda j, i: (i, j))
    return pl.pallas_call(
        body, name="mix_in", grid=(N_CHIP, S // tm),
        in_specs=[pl.BlockSpec((tm, D), lambda j, i: (i, 0)), _wspec(D, Nb, l, 0)],
        out_specs=[out, out],
        out_shape=[jax.ShapeDtypeStruct((S, N_CHIP * Nb), F32), jax.ShapeDtypeStruct((S, N_CHIP * Nb), BF16)],
        compiler_params=_params("arbitrary", "arbitrary"))(h, w_in)


def mix_out_bwd(dx, w_out, l):
    S, D = dx.shape
    Kb = w_out.shape[2]
    tm = _tile(S, 512)

    def body(dx_ref, w_ref, dc_ref):
        dc_ref[...] = _dot_nt(dx_ref[...].astype(BF16), w_ref[...])

    return pl.pallas_call(
        body, name="mix_out_bwd", grid=(N_CHIP, S // tm),
        in_specs=[pl.BlockSpec((tm, D), lambda j, i: (i, 0)), _wspec(Kb, D, l, 0)],
        out_specs=pl.BlockSpec((tm, Kb), lambda j, i: (i, j)),
        out_shape=jax.ShapeDtypeStruct((S, N_CHIP * Kb), F32),
        compiler_params=_params("arbitrary", "arbitrary"))(dx, w_out)


def final_loss(x, g, target):
    S, D = x.shape
    tm = _tile(S, 256)

    def body(x_ref, g_ref, t_ref, loss_ref, dx_ref, dg_ref):
        @pl.when(pl.program_id(0) == 0)
        def _():
            loss_ref[...] = jnp.zeros_like(loss_ref)
            dg_ref[...] = jnp.zeros_like(dg_ref)

        xv = x_ref[...]
        rstd = lax.rsqrt(jnp.mean(xv * xv, axis=-1, keepdims=True) + NORM_EPS)
        xhat = xv * rstd
        err = xhat * g_ref[...] - t_ref[...]
        loss_ref[...] += jnp.sum(err * err, axis=0, keepdims=True)
        dy = err * (1.0 / D)
        gd = dy * g_ref[...]
        dx_ref[...] = rstd * (gd - xhat * jnp.mean(gd * xhat, axis=-1, keepdims=True))
        dg_ref[...] += jnp.sum(dy * xhat, axis=0, keepdims=True)

    row = pl.BlockSpec((tm, D), lambda i: (i, 0))
    vec = pl.BlockSpec((1, D), lambda i: (0, 0))
    return pl.pallas_call(
        body, name="final_loss", grid=(S // tm,), in_specs=[row, vec, row], out_specs=[vec, row, vec],
        out_shape=[jax.ShapeDtypeStruct((1, D), F32), jax.ShapeDtypeStruct((S, D), F32),
                   jax.ShapeDtypeStruct((1, D), F32)],
        compiler_params=_params("arbitrary"))(x, g, target)


def _tri(T, strict_lower):
    row = lax.broadcasted_iota(jnp.int32, (T, T), 0)
    col = lax.broadcasted_iota(jnp.int32, (T, T), 1)
    return jnp.where((row > col) if strict_lower else (row < col), 1.0, 0.0).astype(BF16)


def _causal(rows, cols):
    return lax.broadcasted_iota(jnp.int32, (rows, cols), 1) < lax.broadcasted_iota(jnp.int32, (rows, cols), 0)


def attn_fwd(zb, n_heads, jobs=()):
    S = zb.shape[0]
    TQ = _tile(S, ATTN_TQ_FWD)
    TK = min(ATTN_TK, TQ)
    R = TQ // TK
    scale = HEAD ** -0.5

    def body(q_ref, k_ref, v_ref, o_ref, l_ref):
        i = pl.program_id(1)
        q = q_ref[...]
        later = _tri(TK, True)

        def tile(qr, k, v, carry, acc, mask):
            z = _dot_nt(qr, k) * scale
            lf = _neg_softplus(z)
            if mask is not None:
                lf = jnp.where(mask, lf, 0.0)
            after = _dot(lf.astype(BF16), later)
            w = jnp.exp(z + lf + after + carry)
            if mask is not None:
                w = jnp.where(mask, w, 0.0)
            acc = acc + _dot(w.astype(BF16), v)
            carry = carry + jnp.sum(lf, axis=1, keepdims=True)
            return carry, acc

        def kv(kb):
            start = pl.multiple_of(kb * TK, TK)
            return k_ref[pl.ds(start, TK), :], v_ref[pl.ds(start, TK), :]

        carry, acc = jnp.zeros((TQ, 1), F32), jnp.zeros((TQ, HEAD), F32)
        for r in reversed(range(R)):
            k, v = kv(i * R + r)
            c_r, a_r = tile(q[r * TK:], k, v, carry[r * TK:], acc[r * TK:], _causal(TQ - r * TK, TK))
            carry = c_r if r == 0 else jnp.concatenate([carry[:r * TK], c_r], axis=0)
            acc = a_r if r == 0 else jnp.concatenate([acc[:r * TK], a_r], axis=0)
        carry, acc = lax.fori_loop(0, i * R, lambda n, ca: tile(q, *kv(i * R - 1 - n), ca[0], ca[1], None),
                                   (carry, acc))
        o_ref[...] = acc.astype(BF16)
        l_ref[...] = carry

    H = n_heads
    return _carried_call(
        body, jobs, "attn_fwd", (H, S // TQ),
        [pl.BlockSpec((TQ, HEAD), lambda h, i: (i, h)),
         pl.BlockSpec((S, HEAD), lambda h, i: (0, H + h)),
         pl.BlockSpec((S, HEAD), lambda h, i: (0, 2 * H + h))],
        [pl.BlockSpec((TQ, HEAD), lambda h, i: (i, h)), pl.BlockSpec((None, TQ, 1), lambda h, i: (h, i, 0))],
        [jax.ShapeDtypeStruct((S, H * HEAD), BF16), jax.ShapeDtypeStruct((H, S, 1), F32)],
        [], (zb, zb, zb), ("arbitrary", "arbitrary"))


def attn_bwd(zb, ltot, dc, n_heads, jobs=()):
    S = zb.shape[0]
    TQ = _tile(S, ATTN_TQ)
    TK = min(ATTN_TK, TQ)
    R = TQ // TK
    scale = HEAD ** -0.5

    def body(q_ref, k_ref, v_ref, l_ref, do_ref, dq_ref, dk_ref, dv_ref):
        i = pl.program_id(1)

        @pl.when(i == 0)
        def _():
            dk_ref[...] = jnp.zeros_like(dk_ref)
            dv_ref[...] = jnp.zeros_like(dv_ref)

        q = q_ref[...]
        dob = do_ref[...].astype(BF16)
        ltot_rows = l_ref[...]
        earlier = _tri(TK, False)
        later = _tri(TK, True)

        def tile(kb, r0, c_lf, c_g, dq, mask):
            start = pl.multiple_of(kb * TK, TK)
            k = k_ref[pl.ds(start, TK), :]
            v = v_ref[pl.ds(start, TK), :]
            z = _dot_nt(q[r0:], k) * scale
            lf = _neg_softplus(z)
            if mask is not None:
                lf = jnp.where(mask, lf, 0.0)
            tot = c_lf + jnp.sum(lf, axis=1, keepdims=True)
            w = jnp.exp(z + lf + _dot(lf.astype(BF16), later) + (ltot_rows[r0:] - tot))
            if mask is not None:
                w = jnp.where(mask, w, 0.0)
            g = w * _dot_nt(dob[r0:], v)
            before = _dot(g.astype(BF16), earlier) + c_g
            sig = jnp.exp(z + lf)
            dz = g - sig * (g + before)
            if mask is not None:
                dz = jnp.where(mask, dz, 0.0)
            dzb = dz.astype(BF16)
            dv_ref[pl.ds(start, TK), :] += _dot_tn(w.astype(BF16), dob[r0:])
            dk_ref[pl.ds(start, TK), :] += scale * _dot_tn(dzb, q[r0:])
            dq = dq + scale * _dot(dzb, k)
            return (tot, c_g + jnp.sum(g, axis=1, keepdims=True), dq)

        zero = jnp.zeros((TQ, 1), F32)
        c_lf, c_g, dq = lax.fori_loop(0, i * R, lambda kb, c: tile(kb, 0, c[0], c[1], c[2], None),
                                      (zero, zero, jnp.zeros((TQ, HEAD), F32)))
        for r in range(R):
            r0 = r * TK
            c_r, g_r, dq_r = tile(i * R + r, r0, c_lf[r0:], c_g[r0:], dq[r0:], _causal(TQ - r0, TK))
            if r == 0:
                c_lf, c_g, dq = c_r, g_r, dq_r
            else:
                c_lf = jnp.concatenate([c_lf[:r0], c_r], axis=0)
                c_g = jnp.concatenate([c_g[:r0], g_r], axis=0)
                dq = jnp.concatenate([dq[:r0], dq_r], axis=0)
        dq_ref[...] = dq

    H = n_heads
    full = pl.BlockSpec((S, HEAD), lambda h, i: (0, h))
    blk = pl.BlockSpec((TQ, HEAD), lambda h, i: (i, h))
    shp = jax.ShapeDtypeStruct((S, H * HEAD), F32)
    return _carried_call(
        body, jobs, "attn_bwd", (H, S // TQ),
        [blk, pl.BlockSpec((S, HEAD), lambda h, i: (0, H + h)),
         pl.BlockSpec((S, HEAD), lambda h, i: (0, 2 * H + h)),
         pl.BlockSpec((None, TQ, 1), lambda h, i: (h, i, 0)), blk],
        [blk, full, full], [shp, shp, shp], [], (zb, zb, zb, ltot, dc), ("arbitrary", "arbitrary"))


def _rg_gates(u, ga, gx, lam):
    r = _sigmoid(ga)
    i = _sigmoid(gx)
    log_a = RG_C * r * _neg_softplus(-lam)
    a = jnp.exp(log_a)
    b = jnp.sqrt(-_expm1(2.0 * log_a)) * (i * u)
    return a, b


def _heads_matmul(u, w_ref, transpose):
    outs = []
    for hh in range(w_ref.shape[0]):
        uh = u[:, hh * HEAD:(hh + 1) * HEAD].astype(BF16)
        wv = w_ref[hh].astype(BF16)
        outs.append(_dot_nt(uh, wv) if transpose else _dot(uh, wv))
    return jnp.concatenate(outs, axis=1)


def _conv(buf, cw_ref, cb_ref, tb):
    u = cb_ref[...] + cw_ref[pl.ds(CONV_WIDTH - 1, 1), :] * buf[pl.ds(8, tb), :]
    for k in range(1, CONV_WIDTH):
        u = u + cw_ref[pl.ds(CONV_WIDTH - 1 - k, 1), :] * buf[pl.ds(8 - k, tb), :]
    return u


def _rnn_specs(tb, C, nb, col_gate, rev):
    blk_of = (lambda i: nb - 1 - i) if rev else (lambda i: i)
    blk = lambda col: pl.BlockSpec((tb, C), lambda i: (blk_of(i), col))
    halo = lambda col: pl.BlockSpec((8, C), lambda i: (jnp.maximum(blk_of(i) * (tb // 8) - 1, 0), col))
    return blk, halo


def rnn_fwd(z, cw, cb, wa, ba, wx, bx, lam):
    S = z.shape[0]
    C = cb.shape[-1]
    tb = _tile(S, 512)
    nb = S // tb

    def body(xg_ref, xr_ref, halo_ref, cw_ref, cb_ref, wa_ref, ba_ref, wx_ref, bx_ref, lam_ref,
             out_ref, hs_ref, buf, a_s, b_s, h_c):
        i = pl.program_id(0)

        @pl.when(i == 0)
        def _():
            h_c[...] = jnp.zeros_like(h_c)

        buf[pl.ds(0, 8), :] = jnp.where(i == 0, 0.0, halo_ref[...])
        buf[pl.ds(8, tb), :] = xr_ref[...]
        u = _conv(buf, cw_ref, cb_ref, tb)
        ga = _heads_matmul(u, wa_ref, False) + ba_ref[...]
        gx = _heads_matmul(u, wx_ref, False) + bx_ref[...]
        a, b = _rg_gates(u, ga, gx, lam_ref[...])
        a_s[...] = a
        b_s[...] = b

        def step(t, h):
            h = a_s[pl.ds(t, 1), :] * h + b_s[pl.ds(t, 1), :]
            hs_ref[pl.ds(t, 1), :] = h
            return h

        h_c[...] = lax.fori_loop(0, tb, step, h_c[...], unroll=8)
        out_ref[...] = (_gelu(xg_ref[...]) * hs_ref[...]).astype(BF16)

    blk, halo = _rnn_specs(tb, C, nb, 6, False)
    full = lambda a: pl.BlockSpec(a.shape, lambda i: (0,) * a.ndim)
    out = pl.BlockSpec((tb, C), lambda i: (i, 0))
    return pl.pallas_call(
        body, name="rnn_fwd", grid=(nb,),
        in_specs=[blk(6), blk(7), halo(7)] + [full(a) for a in (cw, cb, wa, ba, wx, bx, lam)],
        out_specs=[out, out],
        out_shape=[jax.ShapeDtypeStruct((S, C), BF16), jax.ShapeDtypeStruct((S, C), F32)],
        scratch_shapes=[pltpu.VMEM((tb + 8, C), F32), pltpu.VMEM((tb, C), F32), pltpu.VMEM((tb, C), F32),
                        pltpu.VMEM((1, C), F32)],
        compiler_params=_params("arbitrary"))(z, z, z, cw, cb, wa, ba, wx, bx, lam)


def rnn_bwd(z, hs, dc, cw, cb, wa, ba, wx, bx, lam):
    S = z.shape[0]
    C = cb.shape[-1]
    tb = _tile(S, 512)
    nb = S // tb

    def body(xg_ref, xr_ref, halo_ref, hs_ref, hs_halo_ref, do_ref,
             cw_ref, cb_ref, wa_ref, ba_ref, wx_ref, bx_ref, lam_ref,
             dxg_ref, dxr_ref, dcw_ref, dcb_ref, dwa_ref, dba_ref, dwx_ref, dbx_ref, dlam_ref,
             buf, hbuf, dubuf, a_s, dh_s, carry):
        i = pl.program_id(0)
        first = i == nb - 1

        @pl.when(i == 0)
        def _():
            carry[...] = jnp.zeros_like(carry)
            dubuf[pl.ds(tb, 8), :] = jnp.zeros((8, C), F32)
            for r in (dcw_ref, dcb_ref, dwa_ref, dba_ref, dwx_ref, dbx_ref, dlam_ref):
                r[...] = jnp.zeros_like(r)

        buf[pl.ds(0, 8), :] = jnp.where(first, 0.0, halo_ref[...])
        buf[pl.ds(8, tb), :] = xr_ref[...]
        u = _conv(buf, cw_ref, cb_ref, tb)
        ga = _heads_matmul(u, wa_ref, False) + ba_ref[...]
        gx = _heads_matmul(u, wx_ref, False) + bx_ref[...]
        (a, _), gates_vjp = jax.vjp(_rg_gates, u, ga, gx, lam_ref[...])
        gel, gelu_vjp = jax.vjp(_gelu, xg_ref[...])
        hs = hs_ref[...]
        do = do_ref[...]
        dxg_ref[...] = gelu_vjp(do * hs)[0].astype(BF16)
        dh_s[...] = do * gel
        a_s[...] = a

        def step(n, c):
            t = tb - 1 - n
            acc = dh_s[pl.ds(t, 1), :] + c
            dh_s[pl.ds(t, 1), :] = acc
            return a_s[pl.ds(t, 1), :] * acc

        carry[...] = lax.fori_loop(0, tb, step, carry[...], unroll=8)
        db = dh_s[...]
        hbuf[pl.ds(0, 8), :] = jnp.where(first, 0.0, hs_halo_ref[...])
        hbuf[pl.ds(8, tb), :] = hs
        da = db * hbuf[pl.ds(7, tb), :]
        du, dga, dgx, dlam = gates_vjp((da, db))
        du = du + _heads_matmul(dga, wa_ref, True) + _heads_matmul(dgx, wx_ref, True)
        dlam_ref[...] += dlam
        dba_ref[...] += jnp.sum(dga, axis=0, keepdims=True)
        dbx_ref[...] += jnp.sum(dgx, axis=0, keepdims=True)
        ub = u.astype(BF16)
        for hh in range(wa_ref.shape[0]):
            cols = slice(hh * HEAD, (hh + 1) * HEAD)
            dwa_ref[hh] += _dot_tn(ub[:, cols], dga[:, cols].astype(BF16))
            dwx_ref[hh] += _dot_tn(ub[:, cols], dgx[:, cols].astype(BF16))
        dcb_ref[...] += jnp.sum(du, axis=0, keepdims=True)
        dubuf[pl.ds(0, tb), :] = du
        dxr = cw_ref[pl.ds(CONV_WIDTH - 1, 1), :] * du
        dcw_ref[pl.ds(CONV_WIDTH - 1, 1), :] += jnp.sum(du * buf[pl.ds(8, tb), :], axis=0, keepdims=True)
        for k in range(1, CONV_WIDTH):
            dxr = dxr + cw_ref[pl.ds(CONV_WIDTH - 1 - k, 1), :] * dubuf[pl.ds(k, tb), :]
            dcw_ref[pl.ds(CONV_WIDTH - 1 - k, 1), :] += jnp.sum(du * buf[pl.ds(8 - k, tb), :], axis=0,
                                                                keepdims=True)
        dxr_ref[...] = dxr.astype(BF16)
        dubuf[pl.ds(tb, 8), :] = du[0:8, :]

    blk, halo = _rnn_specs(tb, C, nb, 6, True)
    full = lambda a: pl.BlockSpec(a.shape, lambda i: (0,) * a.ndim)
    out = pl.BlockSpec((tb, C), lambda i: (nb - 1 - i, 0))
    params = (cw, cb, wa, ba, wx, bx, lam)
    return pl.pallas_call(
        body, name="rnn_bwd", grid=(nb,),
        in_specs=[blk(6), blk(7), halo(7), blk(0), halo(0), blk(2)] + [full(a) for a in params],
        out_specs=[out, out] + [full(a) for a in params],
        out_shape=[jax.ShapeDtypeStruct((S, C), BF16)] * 2 + [jax.ShapeDtypeStruct(a.shape, F32) for a in params],
        scratch_shapes=[pltpu.VMEM((tb + 8, C), F32), pltpu.VMEM((tb + 8, C), F32), pltpu.VMEM((tb + 8, C), F32),
                        pltpu.VMEM((tb, C), F32), pltpu.VMEM((tb, C), F32), pltpu.VMEM((1, C), F32)],
        compiler_params=_params("arbitrary"))(z, z, z, hs, hs, dc, *params)


def _pool_stats(buf, x, t0, tb, n_groups):
    t = t0 + lax.broadcasted_iota(jnp.int32, (tb, 1), 0)
    ds, cnts = [], []
    for g in range(n_groups):
        win = POOL_WINDOWS[g]
        cols = slice(g * HEAD, (g + 1) * HEAD)
        s = buf[pl.ds(16, tb), cols]
        for k in range(1, win):
            s = s + buf[pl.ds(16 - k, tb), cols]
        cnt = jnp.minimum(t + 1, win).astype(F32)
        ds.append(s / cnt - x[:, cols])
        cnts.append(cnt)
    return ds, cnts


def _pool_specs(tb, C, nb, col, rev):
    blk_of = (lambda i: nb - 1 - i) if rev else (lambda i: i)
    blk = pl.BlockSpec((tb, C), lambda i: (blk_of(i), col))
    halo = pl.BlockSpec((16, C), lambda i: (jnp.maximum(blk_of(i) * (tb // 16) - 1, 0), col))
    return blk, halo


def pool_fwd(z, pw, ps):
    S = z.shape[0]
    C = ps.shape[-1]
    G = pw.shape[0]
    tb = _tile(S, 512)
    nb = S // tb

    def body(x_ref, halo_ref, pw_ref, ps_ref, out_ref, buf):
        i = pl.program_id(0)
        buf[pl.ds(0, 16), :] = jnp.where(i == 0, 0.0, halo_ref[...])
        x = x_ref[...]
        buf[pl.ds(16, tb), :] = x
        ds, _ = _pool_stats(buf, x, i * tb, tb, G)
        y = jnp.concatenate([_dot(ds[g].astype(BF16), pw_ref[g].astype(BF16)) for g in range(G)], axis=1)
        out_ref[...] = (y * ps_ref[...]).astype(BF16)

    blk, halo = _pool_specs(tb, C, nb, 8, False)
    full = lambda a: pl.BlockSpec(a.shape, lambda i: (0,) * a.ndim)
    return pl.pallas_call(
        body, name="pool_fwd", grid=(nb,), in_specs=[blk, halo, full(pw), full(ps)],
        out_specs=pl.BlockSpec((tb, C), lambda i: (i, 0)), out_shape=jax.ShapeDtypeStruct((S, C), BF16),
        scratch_shapes=[pltpu.VMEM((tb + 16, C), F32)], compiler_params=_params("arbitrary"))(z, z, pw, ps)


def pool_bwd(z, dc, pw, ps):
    S = z.shape[0]
    C = ps.shape[-1]
    G = pw.shape[0]
    tb = _tile(S, 512)
    nb = S // tb

    def body(x_ref, halo_ref, dy_ref, pw_ref, ps_ref, dx_ref, dpw_ref, dps_ref, buf, ebuf):
        i = pl.program_id(0)
        ib = nb - 1 - i

        @pl.when(i == 0)
        def _():
            ebuf[pl.ds(tb, 16), :] = jnp.zeros((16, C), F32)
            dpw_ref[...] = jnp.zeros_like(dpw_ref)
            dps_ref[...] = jnp.zeros_like(dps_ref)

        buf[pl.ds(0, 16), :] = jnp.where(ib == 0, 0.0, halo_ref[...])
        x = x_ref[...]
        buf[pl.ds(16, tb), :] = x
        ds, cnts = _pool_stats(buf, x, ib * tb, tb, G)
        dy = dy_ref[...]
        dyp = dy * ps_ref[...]
        ypre, dds, es = [], [], []
        for g in range(G):
            cols = slice(g * HEAD, (g + 1) * HEAD)
            db = ds[g].astype(BF16)
            wv = pw_ref[g].astype(BF16)
            dypb = dyp[:, cols].astype(BF16)
            ypre.append(_dot(db, wv))
            dd = _dot_nt(dypb, wv)
            dpw_ref[g] += _dot_tn(db, dypb)
            dds.append(dd)
            es.append(dd / cnts[g])
        dps_ref[...] += jnp.sum(dy * jnp.concatenate(ypre, axis=1), axis=0, keepdims=True)
        e = jnp.concatenate(es, axis=1)
        ebuf[pl.ds(0, tb), :] = e
        dxs = []
        for g in range(G):
            cols = slice(g * HEAD, (g + 1) * HEAD)
            s = es[g]
            for k in range(1, POOL_WINDOWS[g]):
                s = s + ebuf[pl.ds(k, tb), cols]
            dxs.append(s - dds[g])
        dx_ref[...] = jnp.concatenate(dxs, axis=1).astype(BF16)
        ebuf[pl.ds(tb, 16), :] = e[0:16, :]

    blk, halo = _pool_specs(tb, C, nb, 8, True)
    dyb, _ = _pool_specs(tb, C, nb, 3, True)
    full = lambda a: pl.BlockSpec(a.shape, lambda i: (0,) * a.ndim)
    return pl.pallas_call(
        body, name="pool_bwd", grid=(nb,), in_specs=[blk, halo, dyb, full(pw), full(ps)],
        out_specs=[pl.BlockSpec((tb, C), lambda i: (nb - 1 - i, 0)), full(pw), full(ps)],
        out_shape=[jax.ShapeDtypeStruct((S, C), BF16), jax.ShapeDtypeStruct(pw.shape, F32),
                   jax.ShapeDtypeStruct(ps.shape, F32)],
        scratch_shapes=[pltpu.VMEM((tb + 16, C), F32), pltpu.VMEM((tb + 16, C), F32)],
        compiler_params=_params("arbitrary"))(z, z, dc, pw, ps)


def adamw(w, g, m, v):
    R, N = w.shape
    tr = _tile(R, 256)

    def body(w_ref, g_ref, m_ref, v_ref, d_ref, m2_ref, v2_ref):
        gv = g_ref[...]
        m2 = ADAM_B1 * m_ref[...] + (1.0 - ADAM_B1) * gv
        v2 = ADAM_B2 * v_ref[...] + (1.0 - ADAM_B2) * (gv * gv)
        m_hat = m2 / (1.0 - ADAM_B1 ** ADAM_STEP)
        v_hat = v2 / (1.0 - ADAM_B2 ** ADAM_STEP)
        d_ref[...] = -ADAM_LR * (m_hat / (jnp.sqrt(v_hat) + ADAM_EPS) + ADAM_WD * w_ref[...])
        m2_ref[...] = m2
        v2_ref[...] = v2

    blk = pl.BlockSpec((tr, N), lambda i: (i, 0))
    shp = jax.ShapeDtypeStruct((R, N), F32)
    return pl.pallas_call(body, name="adamw", grid=(R // tr,), in_specs=[blk] * 4, out_specs=[blk] * 3,
                          out_shape=[shp] * 3, compiler_params=_params("arbitrary"))(w, g, m, v)


def own_slot(w, dtype, me_arr):
    L, K, N = w.shape
    tk = _tile(K, 512)

    def body(me_ref, w_ref, o_ref):
        o_ref[...] = w_ref[...].astype(dtype)

    return pl.pallas_call(
        body, name="own_slot",
        grid_spec=pltpu.PrefetchScalarGridSpec(
            num_scalar_prefetch=1, grid=(L, K // tk),
            in_specs=[pl.BlockSpec((None, tk, N), lambda l, i, me: (l, i, 0))],
            out_specs=pl.BlockSpec((None, None, tk, N), lambda l, i, me: (me[0], l, i, 0))),
        out_shape=jax.ShapeDtypeStruct((N_CHIP, L, K, N), dtype),
        compiler_params=_params("arbitrary", "arbitrary"))(me_arr, w)


def chip_sum(g, got, c_arr):
    _, K, N = g.shape
    Kh = K // 2
    tr = _tile(Kh, 352 if Kh % 352 == 0 else 256)

    def body(c_ref, g_ref, got_ref, o_ref):
        o_ref[...] = (g_ref[...] + got_ref[...]).astype(BF16)

    nt = Kh // tr
    return pl.pallas_call(
        body, name="chip_sum",
        grid_spec=pltpu.PrefetchScalarGridSpec(
            num_scalar_prefetch=1, grid=(N_CHIP, nt),
            in_specs=[pl.BlockSpec((None, tr, N), lambda j, i, c: (j, c[0] * nt + i, 0)),
                      pl.BlockSpec((None, tr, N), lambda j, i, c: (j, i, 0))],
            out_specs=pl.BlockSpec((None, tr, N), lambda j, i, c: (j, i, 0))),
        out_shape=jax.ShapeDtypeStruct((N_CHIP, Kh, N), BF16),
        compiler_params=_params("arbitrary", "arbitrary"))(c_arr, g, got)


def total_sum(cs, got, me_arr, c_arr, l, into):
    _, Kh, N = cs.shape
    tr = _tile(Kh, 352 if Kh % 352 == 0 else 256)
    nt = Kh // tr

    def body(me_ref, c_ref, cs_ref, got_ref, *rest):
        o_ref = rest[-1]
        o_ref[...] = ((cs_ref[...].astype(F32) + got_ref[0].astype(F32)) + got_ref[1].astype(F32)) + \
            got_ref[2].astype(F32)

    ins = [pl.BlockSpec((None, tr, N), lambda i, me, c: (me[0], i, 0)),
           pl.BlockSpec((3, tr, N), lambda i, me, c: (0, i, 0))]
    args = [me_arr, c_arr, cs, got]
    alias = {}
    if into is not None:
        ins.append(pl.BlockSpec(memory_space=pl.ANY))
        args.append(into)
        alias = {4: 0}
    return pl.pallas_call(
        body, name="total_sum",
        grid_spec=pltpu.PrefetchScalarGridSpec(
            num_scalar_prefetch=2, grid=(nt,), in_specs=ins,
            out_specs=pl.BlockSpec((None, tr, N), lambda i, me, c: (l, c[0] * nt + i, 0))),
        out_shape=jax.ShapeDtypeStruct((2, 2 * Kh, N), F32), input_output_aliases=alias,
        compiler_params=_params("arbitrary"))(*args)


def join_halves(buf):
    Kh = buf.shape[1] // 2

    def body(in_ref, out_ref, send_sem, recv_sem):
        x, y, c, _ = _place()
        mine = out_ref.at[:, pl.ds(c * Kh, Kh), :]
        cp = pltpu.make_async_remote_copy(
            src_ref=mine, dst_ref=mine, send_sem=send_sem, recv_sem=recv_sem,
            device_id=(x, y, 1 - c), device_id_type=MESH)
        cp.start()
        cp.wait()

    return pl.pallas_call(
        body, name="join_halves", in_specs=[HBM], out_specs=HBM,
        out_shape=jax.ShapeDtypeStruct(buf.shape, F32), input_output_aliases={0: 0},
        scratch_shapes=[pltpu.SemaphoreType.DMA, pltpu.SemaphoreType.DMA])(buf)


def allreduce_small(p):
    R = p.shape[0]

    def body(p_ref, out_ref, all_ref, send_sems, recv_sems, local_sem):
        x, y, c, others = _place()
        me, sibling = (x, y, c), (x, y, 1 - c)

        def rows(px, py, pc):
            return all_ref.at[pl.ds((4 * px + 2 * py + pc) * R, R), :]

        def copy(k, block, to, src=None):
            return pltpu.make_async_remote_copy(
                src_ref=rows(*block) if src is None else src, dst_ref=rows(*block), send_sem=send_sems.at[k],
                recv_sem=recv_sems.at[k], device_id=to, device_id_type=MESH)

        mine = pltpu.make_async_copy(p_ref, rows(*me), local_sem)
        mine.start()
        first = [copy(0, me, sibling, src=p_ref)]
        first += [copy(1 + j, me, (*chip, c), src=p_ref) for j, chip in enumerate(others)]
        for cp in first:
            cp.start()
        passed = [copy(4 + j, (*chip, c), sibling) for j, chip in enumerate(others)]
        for j, chip in enumerate(others):
            copy(1 + j, (*chip, c), me).wait_recv()
            passed[j].start()
        copy(0, sibling, me).wait_recv()
        for j, chip in enumerate(others):
            copy(4 + j, (*chip, 1 - c), me).wait_recv()
        for cp in first + passed:
            cp.wait_send()
        mine.wait()
        total = all_ref[pl.ds(0, R), :]
        for d in range(1, N_DEV):
            total = total + all_ref[pl.ds(d * R, R), :]
        out_ref[...] = total

    vmem = pl.BlockSpec(memory_space=pltpu.VMEM)
    return pl.pallas_call(
        body, name="allreduce_small", in_specs=[vmem], out_specs=vmem,
        out_shape=jax.ShapeDtypeStruct((R, 128), F32),
        scratch_shapes=[pltpu.VMEM((N_DEV * R, 128), F32), pltpu.SemaphoreType.DMA((7,)),
                        pltpu.SemaphoreType.DMA((7,)), pltpu.SemaphoreType.DMA],
        compiler_params=pltpu.CompilerParams(vmem_limit_bytes=VMEM_LIMIT_V7X))(p)


def _ffn_fwd(x, g, Wg, names, l, slots, late):
    h = rmsnorm_fwd(x, g)
    (a, b, y), got = ffn_up(h, Wg[names[0]], Wg[names[1]], l, [_gather_job(slots[n]) for n in late])
    for n, o in zip(late, got):
        Wg[n] = o[0]
    return residual_matmul(x, y, Wg[names[2]], l, 0.5, True), (x, h, a, b, y)


class _Reducer:
    def __init__(self, c_arr):
        self.c_arr, self.fresh, self.summed, self.landed = c_arr, [], [], {}

    def jobs(self):
        return [_swap_job(g) for _, _, g in self.fresh] + [_scatter_job(cs) for _, _, cs in self.summed]

    def done(self, got):
        n_swap = len(self.fresh)
        for (n, l, cs), o in zip(self.summed, got[n_swap:]):
            self.landed[n, l] = (cs, o[0])
        self.summed = [(n, l, chip_sum(g, o[0], self.c_arr)) for (n, l, g), o in zip(self.fresh, got[:n_swap])]
        self.fresh = []

    def weight_grad(self, name, l, *args, **kw):
        g, got = weight_grad(*args, jobs=self.jobs(), **kw)
        self.done(got)
        self.fresh.append((name, l, g))

    def flush(self):
        while self.fresh or self.summed:
            self.done(_run_jobs(self.jobs(), "reduce_tail"))


def _ffn_bwd(red, names, dxo, saved, g, Wg, l):
    x, h, a, b, y = saved
    D = x.shape[1]
    wg, wu, wd = (Wg[n] for n in names)
    F = wg.shape[-1]
    da, db = ffn_bwd_hidden(dxo, wd, a, b, l)
    red.weight_grad(names[2], l, y, dxo, F, D, True, False, b_scale=0.5)
    red.weight_grad(names[0], l, h, da, D, F, False, True)
    red.weight_grad(names[1], l, h, db, D, F, False, True)
    (dx, dg), got = norm_bwd_matmul(x, g, dxo, [(da, wg), (db, wu)], l, True, red.jobs())
    red.done(got)
    return dx, dg


def kernel(x, norm_ffn1, ffn1_gate, ffn1_up, ffn1_down, norm_mix, w_in, conv_w, conv_b, rg_w_a, rg_b_a, rg_w_x, rg_b_x, rg_lambda, pool_w, pool_scale, w_out, norm_ffn2, ffn2_gate, ffn2_up, ffn2_down, norm_final, loss_target, m_norm_ffn1, m_ffn1_gate, m_ffn1_up, m_ffn1_down, m_norm_mix, m_w_in, m_conv_w, m_conv_b, m_rg_w_a, m_rg_b_a, m_rg_w_x, m_rg_b_x, m_rg_lambda, m_pool_w, m_pool_scale, m_w_out, m_norm_ffn2, m_ffn2_gate, m_ffn2_up, m_ffn2_down, m_norm_final, v_norm_ffn1, v_ffn1_gate, v_ffn1_up, v_ffn1_down, v_norm_mix, v_w_in, v_conv_w, v_conv_b, v_rg_w_a, v_rg_b_a, v_rg_w_x, v_rg_b_x, v_rg_lambda, v_pool_w, v_pool_scale, v_w_out, v_norm_ffn2, v_ffn2_gate, v_ffn2_up, v_ffn2_down, v_norm_final):
    given = dict(locals())
    W = {n: given[n] for n in WEIGHTS}
    M = {n: given["m_" + n] for n in WEIGHTS}
    V = {n: given["v_" + n] for n in WEIGHTS}
    depth = norm_ffn1.shape[0]
    S, D = x.shape[1], x.shape[2]
    xs = x.reshape(S, D)
    target = loss_target.reshape(S, D)
    C = conv_b.shape[-1]
    n_heads = (D // 2) // HEAD
    cx, cy, cc = lax.axis_index("x"), lax.axis_index("y"), lax.axis_index("c")
    me = 2 * cx + cy
    me_arr = jnp.reshape(me, (1,)).astype(jnp.int32)
    c_arr = jnp.reshape(cc, (1,)).astype(jnp.int32)

    slots = {n: own_slot(W[n], BF16, me_arr) for n in BIG}
    Wg = {n: _run_jobs([_gather_job(slots[n])], "gather_weight")[0][0] for n in ('ffn1_gate', 'ffn1_up')}
    conv_all = _run_jobs([_gather_job(own_slot(conv_w, F32, me_arr))], "gather_weight")[0][0]
    conv_full = jnp.transpose(conv_all, (1, 2, 0, 3)).reshape(depth, CONV_WIDTH, C)

    row = lambda a, l: a[l].reshape(1, -1)

    saved = []
    cur = xs
    for l in range(depth):
        cur, s1 = _ffn_fwd(cur, row(norm_ffn1, l), Wg, ('ffn1_gate', 'ffn1_up', 'ffn1_down'), l, slots,
                           ['ffn1_down', 'w_in'] if l == 0 else [])
        x1 = cur
        hm = rmsnorm_fwd(x1, row(norm_mix, l))
        z, zb = mix_in(hm, Wg['w_in'], l)
        late = ['w_out', 'ffn2_gate', 'ffn2_up', 'ffn2_down'] if l == 0 else []
        (att, ltot), got = attn_fwd(zb, n_heads, [_gather_job(slots[n]) for n in late])
        for n, o in zip(late, got):
            Wg[n] = o[0]
        rnn_p = (conv_full[l], row(conv_b, l), rg_w_a[l], row(rg_b_a, l), rg_w_x[l], row(rg_b_x, l),
                 row(rg_lambda, l))
        rnn, hs = rnn_fwd(z, *rnn_p)
        pool = pool_fwd(z, pool_w[l], row(pool_scale, l))
        cat = jnp.concatenate([att, rnn, pool], axis=1)
        cur = residual_matmul(x1, cat, Wg['w_out'], l, 1.0, False)
        cur, s2 = _ffn_fwd(cur, row(norm_ffn2, l), Wg, ('ffn2_gate', 'ffn2_up', 'ffn2_down'), l, slots, [])
        saved.append((s1, (x1, hm, z, zb, ltot, hs, cat, rnn_p), s2))

    loss_cols, dx, g_norm_final = final_loss(cur, norm_final.reshape(1, D), target)

    G = {n: [None] * depth for n in SMALL if n != 'norm_final'}
    red = _Reducer(c_arr)
    for l in reversed(range(depth)):
        s1, (x1, hm, z, zb, ltot, hs, cat, rnn_p), s2 = saved[l]
        dx, G['norm_ffn2'][l] = _ffn_bwd(red, ('ffn2_gate', 'ffn2_up', 'ffn2_down'), dx, s2, row(norm_ffn2, l), Wg, l)
        dc = mix_out_bwd(dx, Wg['w_out'], l)
        red.weight_grad('w_out', l, cat, dx, C, D, False, False)
        (dq, dk, dv), got = attn_bwd(zb, ltot, dc, n_heads, red.jobs())
        red.done(got)
        (dxg, dxr, G['conv_w'][l], G['conv_b'][l], G['rg_w_a'][l], G['rg_b_a'][l], G['rg_w_x'][l],
         G['rg_b_x'][l], G['rg_lambda'][l]) = rnn_bwd(z, hs, dc, *rnn_p)
        dxp, G['pool_w'][l], G['pool_scale'][l] = pool_bwd(z, dc, pool_w[l], row(pool_scale, l))
        dz = jnp.concatenate([dq.astype(BF16), dk.astype(BF16), dv.astype(BF16), dxg, dxr, dxp], axis=1)
        red.weight_grad('w_in', l, hm, dz, D, dz.shape[1] // N_CHIP, False, False)
        (dx, G['norm_mix'][l]), got = norm_bwd_matmul(x1, row(norm_mix, l), dx, [(dz, Wg['w_in'])], l, False,
                                                      red.jobs())
        red.done(got)
        dx, G['norm_ffn1'][l] = _ffn_bwd(red, ('ffn1_gate', 'ffn1_up', 'ffn1_down'), dx, s1, row(norm_ffn1, l), Wg, l)
    grad_x = dx.reshape(x.shape)
    red.flush()

    outs = {}
    for n in BIG:
        halves = None
        for l in range(depth):
            cs, got = red.landed[n, l]
            halves = total_sum(cs, got, me_arr, c_arr, l, halves)
        g_red = join_halves(halves)
        two_d = (-1, g_red.shape[-1])
        d, m2, v2 = adamw(W[n].reshape(two_d), g_red.reshape(two_d), M[n].reshape(two_d), V[n].reshape(two_d))
        outs[n] = (g_red, d.reshape(W[n].shape), m2.reshape(W[n].shape), v2.reshape(W[n].shape))

    loss_part = 0.5 * jnp.sum(loss_cols) / D
    parts = [jnp.stack(G[n]).reshape(-1) for n in SMALL if n != 'norm_final']
    parts += [g_norm_final.reshape(-1), jnp.reshape(loss_part, (1,))]
    sizes = [p.shape[0] for p in parts]
    total = sum(sizes)
    R = -(-total // 1024) * 8
    packed = jnp.concatenate(parts + [jnp.zeros((R * 128 - total,), F32)]).reshape(R, 128)
    red = allreduce_small(packed).reshape(-1)
    offs = [sum(sizes[:i]) for i in range(len(sizes))]
    small_names = [n for n in SMALL if n != 'norm_final'] + ['norm_final']
    small_g = {}
    for n, o, sz in zip(small_names, offs, sizes):
        if n == 'conv_w':
            full = red[o:o + sz].reshape(depth, CONV_WIDTH, C)
            small_g[n] = lax.dynamic_slice_in_dim(full, me * (C // N_CHIP), C // N_CHIP, axis=2)
        else:
            small_g[n] = red[o:o + sz].reshape(W[n].shape)
    loss = red[offs[-1]]

    def pack(d):
        flat = jnp.concatenate([d[n].reshape(-1) for n in small_names])
        rows = -(-flat.shape[0] // 1024) * 8
        return jnp.concatenate([flat, jnp.ones((rows * 128 - flat.shape[0],), F32)]).reshape(rows, 128)

    d_s, m_s, v_s = adamw(pack(W), pack(small_g), pack(M), pack(V))
    o = 0
    for n in small_names:
        sz = W[n].size
        outs[n] = (small_g[n],) + tuple(a.reshape(-1)[o:o + sz].reshape(W[n].shape) for a in (d_s, m_s, v_s))
        o += sz

    return (loss, grad_x, *[outs[n][0] for n in WEIGHTS], *[outs[n][1] for n in WEIGHTS],
            *[outs[n][2] for n in WEIGHTS], *[outs[n][3] for n in WEIGHTS])
```

```python
import functools

import jax
import jax.numpy as jnp
from jax import lax
from jax.experimental import pallas as pl
from jax.experimental.pallas import tpu as pltpu

F32 = jnp.float32
BF16 = jnp.bfloat16
MESH = pl.DeviceIdType.MESH

HEAD = 128
ATTN_TQ, ATTN_TK = 1024, 256
ATTN_TQ_FWD = 2048
POOL_WINDOWS = (2, 4, 8, 16)
CONV_WIDTH = 4
RG_C = 8.0
NORM_EPS = 1e-6
N_CHIP = 4
N_DEV = 8
ADAM_LR, ADAM_B1, ADAM_B2, ADAM_EPS, ADAM_WD, ADAM_STEP = 0.001, 0.9, 0.999, 1e-08, 0.01, 10
VMEM_LIMIT_V7X = 56 * 2 ** 20

WEIGHTS = ['norm_ffn1', 'ffn1_gate', 'ffn1_up', 'ffn1_down', 'norm_mix', 'w_in', 'conv_w', 'conv_b', 'rg_w_a',
           'rg_b_a', 'rg_w_x', 'rg_b_x', 'rg_lambda', 'pool_w', 'pool_scale', 'w_out', 'norm_ffn2', 'ffn2_gate',
           'ffn2_up', 'ffn2_down', 'norm_final']
BIG = ['ffn1_gate', 'ffn1_up', 'ffn1_down', 'w_in', 'w_out', 'ffn2_gate', 'ffn2_up', 'ffn2_down']
SMALL = [n for n in WEIGHTS if n not in BIG]


def _params(*sem, vmem=None):
    return pltpu.CompilerParams(dimension_semantics=sem, vmem_limit_bytes=vmem or VMEM_LIMIT_V7X)


def _tile(n, pref):
    t = min(n, pref)
    while n % t:
        t //= 2
    return t


def _dot(a, b):
    return jnp.dot(a, b, preferred_element_type=F32)


def _dot_nt(a, b):
    return lax.dot_general(a, b, (((1,), (1,)), ((), ())), preferred_element_type=F32)


def _dot_tn(a, b):
    return lax.dot_general(a, b, (((0,), (0,)), ((), ())), preferred_element_type=F32)


def _sigmoid(x):
    return 1.0 / (1.0 + jnp.exp(-x))


def _gelu(x):
    return 0.5 * x * (1.0 + jnp.tanh(0.7978845608028654 * (x + 0.044715 * x * x * x)))


def _expm1(x):
    series = x * (1.0 + x * 0.5 * (1.0 + x * (1.0 / 3.0) * (1.0 + x * 0.25 * (1.0 + x * 0.2))))
    return jnp.where(jnp.abs(x) < 0.1, series, jnp.exp(x) - 1.0)


def _neg_softplus(z):
    return -(jnp.maximum(z, 0.0) + jnp.log(1.0 + jnp.exp(-jnp.abs(z))))


def _wspec(k, n, l, jpos):
    return pl.BlockSpec((None, None, k, n), lambda *g: (g[jpos], l, 0, 0))


HBM = pl.BlockSpec(memory_space=pltpu.HBM)


def _place():
    x, y, c = lax.axis_index("x"), lax.axis_index("y"), lax.axis_index("c")
    others = [(1 - x, y), (x, 1 - y), (1 - x, 1 - y)]
    return x, y, c, others


def _scatter_job(cs):
    def copies(ins, outs, sems):
        x, y, c, others = _place()
        return [pltpu.make_async_remote_copy(
            src_ref=ins[0].at[2 * chip[0] + chip[1]], dst_ref=outs[0].at[k], send_sem=sems[0].at[k],
            recv_sem=sems[1].at[k], device_id=(*chip, c), device_id_type=MESH) for k, chip in enumerate(others)]

    def start(ins, outs, sems):
        for cp in copies(ins, outs, sems):
            cp.start()

    def finish(ins, outs, sems):
        for cp in copies(ins, outs, sems):
            cp.wait()

    return dict(ins=[cs], outs=[jax.ShapeDtypeStruct((3,) + cs.shape[1:], cs.dtype)], alias={},
                sems=[pltpu.SemaphoreType.DMA((3,)), pltpu.SemaphoreType.DMA((3,))], start=start, finish=finish)


def _swap_job(g):
    Kh = g.shape[1] // 2

    def copy(ins, outs, sems):
        x, y, c, _ = _place()
        return pltpu.make_async_remote_copy(
            src_ref=ins[0].at[:, pl.ds((1 - c) * Kh, Kh), :], dst_ref=outs[0], send_sem=sems[0], recv_sem=sems[1],
            device_id=(x, y, 1 - c), device_id_type=MESH)

    return dict(ins=[g], outs=[jax.ShapeDtypeStruct((N_CHIP, Kh, g.shape[2]), g.dtype)], alias={},
                sems=[pltpu.SemaphoreType.DMA, pltpu.SemaphoreType.DMA],
                start=lambda ins, outs, sems: copy(ins, outs, sems).start(),
                finish=lambda ins, outs, sems: copy(ins, outs, sems).wait())


def _gather_job(buf):
    def copy(outs, sems, k, chip, layer, to):
        dst = outs[0].at[2 * chip[0] + chip[1], layer]
        return pltpu.make_async_remote_copy(src_ref=dst, dst_ref=dst, send_sem=sems[0].at[k], recv_sem=sems[1].at[k],
                                            device_id=to, device_id_type=MESH)

    def start(ins, outs, sems):
        x, y, c, others = _place()
        for k, chip in enumerate(others):
            copy(outs, sems, k, (x, y), c, (*chip, c)).start()

    def finish(ins, outs, sems):
        x, y, c, others = _place()
        me, sibling = (x, y, c), (x, y, 1 - c)
        for k, chip in enumerate(others):
            copy(outs, sems, k, chip, c, me).wait_recv()
            copy(outs, sems, 3 + k, chip, c, sibling).start()
        for k, chip in enumerate(others):
            copy(outs, sems, 3 + k, chip, 1 - c, me).wait_recv()
        for k in range(6):
            copy(outs, sems, k, (x, y), c, me).wait_send()

    return dict(ins=[buf], outs=[jax.ShapeDtypeStruct(buf.shape, buf.dtype)], alias={0: 0},
                sems=[pltpu.SemaphoreType.DMA((6,)), pltpu.SemaphoreType.DMA((6,))], start=start, finish=finish)


def _carried_call(body, jobs, name, grid, in_specs, out_specs, out_shape, scratch_shapes, args, sem, vmem=None):
    n_in, n_out, n_scr = len(in_specs), len(out_specs), len(scratch_shapes)
    j_in = [a for j in jobs for a in j['ins']]
    j_out = [o for j in jobs for o in j['outs']]
    j_sem = [s for j in jobs for s in j['sems']]

    def wrapped(*refs):
        own_in, ji = refs[:n_in], list(refs[n_in:n_in + len(j_in)])
        rest = refs[n_in + len(j_in):]
        own_out, jo = rest[:n_out], list(rest[n_out:n_out + len(j_out)])
        rest = rest[n_out + len(j_out):]
        own_scr, js = rest[:n_scr], list(rest[n_scr:])
        per_job = []
        for j in jobs:
            per_job.append((j, [ji.pop(0) for _ in j['ins']], [jo.pop(0) for _ in j['outs']],
                            [js.pop(0) for _ in j['sems']]))
        ids = [pl.program_id(d) for d in range(len(grid))]
        first = functools.reduce(lambda a, b: a & b, [i == 0 for i in ids])
        last = functools.reduce(lambda a, b: a & b, [i == g - 1 for i, g in zip(ids, grid)])

        if jobs:
            @pl.when(first)
            def _():
                for j, a, o, s in per_job:
                    j['start'](a, o, s)

        body(*own_in, *own_out, *own_scr)

        if jobs:
            @pl.when(last)
            def _():
                for j, a, o, s in per_job:
                    j['finish'](a, o, s)

    alias, off_in, off_out = {}, n_in, n_out
    for j in jobs:
        for li, lo in j['alias'].items():
            alias[off_in + li] = off_out + lo
        off_in += len(j['ins'])
        off_out += len(j['outs'])
    res = pl.pallas_call(
        wrapped, name=name, grid=grid, in_specs=list(in_specs) + [HBM] * len(j_in),
        out_specs=list(out_specs) + [HBM] * len(j_out), out_shape=list(out_shape) + j_out,
        scratch_shapes=list(scratch_shapes) + j_sem, input_output_aliases=alias,
        compiler_params=_params(*sem, vmem=vmem))(*args, *j_in)
    own, rest = list(res[:n_out]), list(res[n_out:])
    return own, [[rest.pop(0) for _ in j['outs']] for j in jobs]


def _run_jobs(jobs, name):
    def body(*refs):
        ins, rest = list(refs[:n_in]), list(refs[n_in:])
        outs, sems = rest[:n_out], rest[n_out:]
        per_job = [(j, [ins.pop(0) for _ in j['ins']], [outs.pop(0) for _ in j['outs']],
                    [sems.pop(0) for _ in j['sems']]) for j in jobs]
        for j, a, o, s in per_job:
            j['start'](a, o, s)
        for j, a, o, s in per_job:
            j['finish'](a, o, s)

    j_in = [a for j in jobs for a in j['ins']]
    j_out = [o for j in jobs for o in j['outs']]
    n_in, n_out = len(j_in), len(j_out)
    alias, off_in, off_out = {}, 0, 0
    for j in jobs:
        for li, lo in j['alias'].items():
            alias[off_in + li] = off_out + lo
        off_in += len(j['ins'])
        off_out += len(j['outs'])
    res = list(pl.pallas_call(
        body, name=name, in_specs=[HBM] * n_in, out_specs=[HBM] * n_out, out_shape=j_out,
        scratch_shapes=[s for j in jobs for s in j['sems']], input_output_aliases=alias)(*j_in))
    return [[res.pop(0) for _ in j['outs']] for j in jobs]


def rmsnorm_fwd(x, g):
    S, D = x.shape
    tm = _tile(S, 512)

    def body(x_ref, g_ref, h_ref):
        xv = x_ref[...]
        rstd = lax.rsqrt(jnp.mean(xv * xv, axis=-1, keepdims=True) + NORM_EPS)
        h_ref[...] = (xv * rstd * g_ref[...]).astype(BF16)

    return pl.pallas_call(
        body, name="rmsnorm_fwd", grid=(S // tm,),
        in_specs=[pl.BlockSpec((tm, D), lambda i: (i, 0)), pl.BlockSpec((1, D), lambda i: (0, 0))],
        out_specs=pl.BlockSpec((tm, D), lambda i: (i, 0)),
        out_shape=jax.ShapeDtypeStruct((S, D), BF16), compiler_params=_params("arbitrary"))(x, g)


def ffn_up(h, wg, wu, l, jobs=()):
    S, D = h.shape
    F = wg.shape[-1]
    tm = _tile(S, 512)

    def body(h_ref, wg_ref, wu_ref, a_ref, b_ref, y_ref):
        hv = h_ref[...]
        a = _dot(hv, wg_ref[...])
        b = _dot(hv, wu_ref[...])
        a_ref[...] = a.astype(BF16)
        b_ref[...] = b.astype(BF16)
        y_ref[...] = (a * _sigmoid(a) * b).astype(BF16)

    out = pl.BlockSpec((None, tm, F), lambda j, i: (j, i, 0))
    shp = jax.ShapeDtypeStruct((N_CHIP, S, F), BF16)
    return _carried_call(
        body, jobs, "ffn_up", (N_CHIP, S // tm),
        [pl.BlockSpec((tm, D), lambda j, i: (i, 0)), _wspec(D, F, l, 0), _wspec(D, F, l, 0)],
        [out, out, out], [shp, shp, shp], [], (h, wg, wu), ("arbitrary", "arbitrary"))


def residual_matmul(x, a, w, l, scale, a_blocked):
    S, D = x.shape
    Kb = w.shape[2]
    tm = _tile(S, 512)
    dc = _tile(D, 512)

    def body(x_ref, a_ref, w_ref, o_ref):
        a_blocks = [a_ref[j] if a_blocked else a_ref[:, pl.ds(j * Kb, Kb)] for j in range(N_CHIP)]
        for c0 in range(0, D, dc):
            cols = pl.ds(c0, dc)
            tot = _dot(a_blocks[0], w_ref[0, :, cols])
            for j in range(1, N_CHIP):
                tot = tot + _dot(a_blocks[j], w_ref[j, :, cols])
            o_ref[:, cols] = x_ref[:, cols] + scale * tot

    a_spec = (pl.BlockSpec((N_CHIP, tm, Kb), lambda i: (0, i, 0)) if a_blocked
              else pl.BlockSpec((tm, N_CHIP * Kb), lambda i: (i, 0)))
    w_spec = pl.BlockSpec((N_CHIP, None, Kb, D), lambda i: (0, l, 0, 0), pipeline_mode=pl.Buffered(1))
    return pl.pallas_call(
        body, name="residual_matmul", grid=(S // tm,),
        in_specs=[pl.BlockSpec((tm, D), lambda i: (i, 0)), a_spec, w_spec],
        out_specs=pl.BlockSpec((tm, D), lambda i: (i, 0)),
        out_shape=jax.ShapeDtypeStruct((S, D), F32),
        compiler_params=_params("arbitrary"))(x, a, w)


def norm_bwd_matmul(x, g, dxo, pairs, l, g_blocked, jobs=()):
    S, D = x.shape
    Nb = pairs[0][1].shape[-1]
    tm = _tile(S, 256)
    n_p = len(pairs)

    def body(*refs):
        x_ref, g_ref, dxo_ref = refs[:3]
        g_refs = refs[3:3 + n_p]
        w_refs = refs[3 + n_p:3 + 2 * n_p]
        dx_ref, dg_ref, acc = refs[3 + 2 * n_p:]
        i, j = pl.program_id(0), pl.program_id(1)

        @pl.when(j == 0)
        def _():
            acc[...] = jnp.zeros_like(acc)

        @pl.when((i == 0) & (j == 0))
        def _():
            dg_ref[...] = jnp.zeros_like(dg_ref)

        for g_r, w_r in zip(g_refs, w_refs):
            acc[...] += _dot_nt(g_r[...], w_r[...])

        @pl.when(j == N_CHIP - 1)
        def _():
            xv = x_ref[...]
            rstd = lax.rsqrt(jnp.mean(xv * xv, axis=-1, keepdims=True) + NORM_EPS)
            xhat = xv * rstd
            dh = acc[...]
            gd = dh * g_ref[...]
            dx_ref[...] = dxo_ref[...] + rstd * (gd - xhat * jnp.mean(gd * xhat, axis=-1, keepdims=True))
            dg_ref[...] += jnp.sum(dh * xhat, axis=0, keepdims=True)

    row = pl.BlockSpec((tm, D), lambda i, j: (i, 0))
    vec = pl.BlockSpec((1, D), lambda i, j: (0, 0))
    g_spec = (pl.BlockSpec((None, tm, Nb), lambda i, j: (j, i, 0)) if g_blocked
              else pl.BlockSpec((tm, Nb), lambda i, j: (i, j)))
    return _carried_call(
        body, jobs, "norm_bwd_matmul", (S // tm, N_CHIP),
        [row, vec, row] + [g_spec] * n_p + [_wspec(D, Nb, l, 1)] * n_p, [row, vec],
        [jax.ShapeDtypeStruct((S, D), F32), jax.ShapeDtypeStruct((1, D), F32)], [pltpu.VMEM((tm, D), F32)],
        (x, g, dxo, *[p[0] for p in pairs], *[p[1] for p in pairs]), ("arbitrary", "arbitrary"))


def ffn_bwd_hidden(dxo, wd, a, b, l):
    S, D = dxo.shape
    F = wd.shape[2]
    tm = _tile(S, 512)

    def body(dxo_ref, wd_ref, a_ref, b_ref, da_ref, db_ref):
        dy = 0.5 * _dot_nt(dxo_ref[...].astype(BF16), wd_ref[...])
        av = a_ref[...].astype(F32)
        bv = b_ref[...].astype(F32)
        s = _sigmoid(av)
        da_ref[...] = (dy * bv * (s * (1.0 + av * (1.0 - s)))).astype(BF16)
        db_ref[...] = (dy * (av * s)).astype(BF16)

    blk = pl.BlockSpec((None, tm, F), lambda j, i: (j, i, 0))
    shp = jax.ShapeDtypeStruct((N_CHIP, S, F), BF16)
    return pl.pallas_call(
        body, name="ffn_bwd_hidden", grid=(N_CHIP, S // tm),
        in_specs=[pl.BlockSpec((tm, D), lambda j, i: (i, 0)), _wspec(F, D, l, 0), blk, blk],
        out_specs=[blk, blk], out_shape=[shp, shp],
        compiler_params=_params("arbitrary", "arbitrary"))(dxo, wd, a, b)


def weight_grad(a, b, m, n, a_blocked, b_blocked, b_scale=1.0, jobs=()):
    S = a.shape[-2]
    ts = _tile(S, 1024)
    tm = m if m % 256 else _tile(m, 1024)
    tn = n if n % 256 else _tile(n, 1024)
    if m * n * 4 <= 6 * 2 ** 20:
        tm, tn = m, n

    def spec(arr, blocked, width, t, pos):
        if blocked:
            return pl.BlockSpec((None, ts, t), lambda j, mi, ni, si: (j, si, (mi, ni)[pos]))
        if arr.shape[-1] == width:
            return pl.BlockSpec((ts, t), lambda j, mi, ni, si: (si, (mi, ni)[pos]))
        nblk = width // t
        return pl.BlockSpec((ts, t), lambda j, mi, ni, si: (si, j * nblk + (mi, ni)[pos]))

    def body(a_ref, b_ref, o_ref):
        @pl.when(pl.program_id(3) == 0)
        def _():
            o_ref[...] = jnp.zeros_like(o_ref)

        bv = b_ref[...]
        if b_scale != 1.0:
            bv = b_scale * bv
        o_ref[...] += _dot_tn(a_ref[...], bv.astype(BF16))

    (out,), got = _carried_call(
        body, jobs, "weight_grad", (N_CHIP, m // tm, n // tn, S // ts),
        [spec(a, a_blocked, m, tm, 0), spec(b, b_blocked, n, tn, 1)],
        [pl.BlockSpec((None, tm, tn), lambda j, mi, ni, si: (j, mi, ni))],
        [jax.ShapeDtypeStruct((N_CHIP, m, n), F32)], [], (a, b),
        ("arbitrary", "arbitrary", "arbitrary", "arbitrary"))
    return out, got


def mix_in(h, w_in, l):
    S, D = h.shape
    Nb = w_in.shape[-1]
    tm = _tile(S, 512)

    def body(h_ref, w_ref, z_ref, zb_ref):
        z = _dot(h_ref[...], w_ref[...])
        z_ref[...] = z
        zb_ref[...] = z.astype(BF16)

    out = pl.BlockSpec((tm, Nb), lambda j, i: (i, j))
    return pl.pallas_call(
        body, name="mix_in", grid=(N_CHIP, S // tm),
        in_specs=[pl.BlockSpec((tm, D), lambda j, i: (i, 0)), _wspec(D, Nb, l, 0)],
        out_specs=[out, out],
        out_shape=[jax.ShapeDtypeStruct((S, N_CHIP * Nb), F32), jax.ShapeDtypeStruct((S, N_CHIP * Nb), BF16)],
        compiler_params=_params("arbitrary", "arbitrary"))(h, w_in)


def mix_out_bwd(dx, w_out, l):
    S, D = dx.shape
    Kb = w_out.shape[2]
    tm = _tile(S, 512)

    def body(dx_ref, w_ref, dc_ref):
        dc_ref[...] = _dot_nt(dx_ref[...].astype(BF16), w_ref[...])

    return pl.pallas_call(
        body, name="mix_out_bwd", grid=(N_CHIP, S // tm),
        in_specs=[pl.BlockSpec((tm, D), lambda j, i: (i, 0)), _wspec(Kb, D, l, 0)],
        out_specs=pl.BlockSpec((tm, Kb), lambda j, i: (i, j)),
        out_shape=jax.ShapeDtypeStruct((S, N_CHIP * Kb), F32),
        compiler_params=_params("arbitrary", "arbitrary"))(dx, w_out)


def final_loss(x, g, target):
    S, D = x.shape
    tm = _tile(S, 256)

    def body(x_ref, g_ref, t_ref, loss_ref, dx_ref, dg_ref):
        @pl.when(pl.program_id(0) == 0)
        def _():
            loss_ref[...] = jnp.zeros_like(loss_ref)
            dg_ref[...] = jnp.zeros_like(dg_ref)

        xv = x_ref[...]
        rstd = lax.rsqrt(jnp.mean(xv * xv, axis=-1, keepdims=True) + NORM_EPS)
        xhat = xv * rstd
        err = xhat * g_ref[...] - t_ref[...]
        loss_ref[...] += jnp.sum(err * err, axis=0, keepdims=True)
        dy = err * (1.0 / D)
        gd = dy * g_ref[...]
        dx_ref[...] = rstd * (gd - xhat * jnp.mean(gd * xhat, axis=-1, keepdims=True))
        dg_ref[...] += jnp.sum(dy * xhat, axis=0, keepdims=True)

    row = pl.BlockSpec((tm, D), lambda i: (i, 0))
    vec = pl.BlockSpec((1, D), lambda i: (0, 0))
    return pl.pallas_call(
        body, name="final_loss", grid=(S // tm,), in_specs=[row, vec, row], out_specs=[vec, row, vec],
        out_shape=[jax.ShapeDtypeStruct((1, D), F32), jax.ShapeDtypeStruct((S, D), F32),
                   jax.ShapeDtypeStruct((1, D), F32)],
        compiler_params=_params("arbitrary"))(x, g, target)


def _tri(T, strict_lower):
    row = lax.broadcasted_iota(jnp.int32, (T, T), 0)
    col = lax.broadcasted_iota(jnp.int32, (T, T), 1)
    return jnp.where((row > col) if strict_lower else (row < col), 1.0, 0.0).astype(BF16)


def _causal(rows, cols):
    return lax.broadcasted_iota(jnp.int32, (rows, cols), 1) < lax.broadcasted_iota(jnp.int32, (rows, cols), 0)


def attn_fwd(zb, n_heads, jobs=()):
    S = zb.shape[0]
    TQ = _tile(S, ATTN_TQ_FWD)
    TK = min(ATTN_TK, TQ)
    R = TQ // TK
    scale = HEAD ** -0.5

    def body(q_ref, k_ref, v_ref, o_ref, l_ref):
        i = pl.program_id(1)
        q = q_ref[...]
        later = _tri(TK, True)

        def tile(qr, k, v, carry, acc, mask):
            z = _dot_nt(qr, k) * scale
            lf = _neg_softplus(z)
            if mask is not None:
                lf = jnp.where(mask, lf, 0.0)
            after = _dot(lf.astype(BF16), later)
            w = jnp.exp(z + lf + after + carry)
            if mask is not None:
                w = jnp.where(mask, w, 0.0)
            acc = acc + _dot(w.astype(BF16), v)
            carry = carry + jnp.sum(lf, axis=1, keepdims=True)
            return carry, acc

        def kv(kb):
            start = pl.multiple_of(kb * TK, TK)
            return k_ref[pl.ds(start, TK), :], v_ref[pl.ds(start, TK), :]

        carry, acc = jnp.zeros((TQ, 1), F32), jnp.zeros((TQ, HEAD), F32)
        for r in reversed(range(R)):
            k, v = kv(i * R + r)
            c_r, a_r = tile(q[r * TK:], k, v, carry[r * TK:], acc[r * TK:], _causal(TQ - r * TK, TK))
            carry = c_r if r == 0 else jnp.concatenate([carry[:r * TK], c_r], axis=0)
            acc = a_r if r == 0 else jnp.concatenate([acc[:r * TK], a_r], axis=0)
        carry, acc = lax.fori_loop(0, i * R, lambda n, ca: tile(q, *kv(i * R - 1 - n), ca[0], ca[1], None),
                                   (carry, acc))
        o_ref[...] = acc.astype(BF16)
        l_ref[...] = carry

    H = n_heads
    return _carried_call(
        body, jobs, "attn_fwd", (H, S // TQ),
        [pl.BlockSpec((TQ, HEAD), lambda h, i: (i, h)),
         pl.BlockSpec((S, HEAD), lambda h, i: (0, H + h)),
         pl.BlockSpec((S, HEAD), lambda h, i: (0, 2 * H + h))],
        [pl.BlockSpec((TQ, HEAD), lambda h, i: (i, h)), pl.BlockSpec((None, TQ, 1), lambda h, i: (h, i, 0))],
        [jax.ShapeDtypeStruct((S, H * HEAD), BF16), jax.ShapeDtypeStruct((H, S, 1), F32)],
        [], (zb, zb, zb), ("arbitrary", "arbitrary"))


def attn_bwd(zb, ltot, dc, n_heads, jobs=()):
    S = zb.shape[0]
    TQ = _tile(S, ATTN_TQ)
    TK = min(ATTN_TK, TQ)
    R = TQ // TK
    scale = HEAD ** -0.5

    def body(q_ref, k_ref, v_ref, l_ref, do_ref, dq_ref, dk_ref, dv_ref):
        i = pl.program_id(1)

        @pl.when(i == 0)
        def _():
            dk_ref[...] = jnp.zeros_like(dk_ref)
            dv_ref[...] = jnp.zeros_like(dv_ref)

        q = q_ref[...]
        dob = do_ref[...].astype(BF16)
        ltot_rows = l_ref[...]
        earlier = _tri(TK, False)
        later = _tri(TK, True)

        def tile(kb, r0, c_lf, c_g, dq, mask):
            start = pl.multiple_of(kb * TK, TK)
            k = k_ref[pl.ds(start, TK), :]
            v = v_ref[pl.ds(start, TK), :]
            z = _dot_nt(q[r0:], k) * scale
            lf = _neg_softplus(z)
            if mask is not None:
                lf = jnp.where(mask, lf, 0.0)
            tot = c_lf + jnp.sum(lf, axis=1, keepdims=True)
            w = jnp.exp(z + lf + _dot(lf.astype(BF16), later) + (ltot_rows[r0:] - tot))
            if mask is not None:
                w = jnp.where(mask, w, 0.0)
            g = w * _dot_nt(dob[r0:], v)
            before = _dot(g.astype(BF16), earlier) + c_g
            sig = jnp.exp(z + lf)
            dz = g - sig * (g + before)
            if mask is not None:
                dz = jnp.where(mask, dz, 0.0)
            dzb = dz.astype(BF16)
            dv_ref[pl.ds(start, TK), :] += _dot_tn(w.astype(BF16), dob[r0:])
            dk_ref[pl.ds(start, TK), :] += scale * _dot_tn(dzb, q[r0:])
            dq = dq + scale * _dot(dzb, k)
            return (tot, c_g + jnp.sum(g, axis=1, keepdims=True), dq)

        zero = jnp.zeros((TQ, 1), F32)
        c_lf, c_g, dq = lax.fori_loop(0, i * R, lambda kb, c: tile(kb, 0, c[0], c[1], c[2], None),
                                      (zero, zero, jnp.zeros((TQ, HEAD), F32)))
        for r in range(R):
            r0 = r * TK
            c_r, g_r, dq_r = tile(i * R + r, r0, c_lf[r0:], c_g[r0:], dq[r0:], _causal(TQ - r0, TK))
            if r == 0:
                c_lf, c_g, dq = c_r, g_r, dq_r
            else:
                c_lf = jnp.concatenate([c_lf[:r0], c_r], axis=0)
                c_g = jnp.concatenate([c_g[:r0], g_r], axis=0)
                dq = jnp.concatenate([dq[:r0], dq_r], axis=0)
        dq_ref[...] = dq

    H = n_heads
    full = pl.BlockSpec((S, HEAD), lambda h, i: (0, h))
    blk = pl.BlockSpec((TQ, HEAD), lambda h, i: (i, h))
    shp = jax.ShapeDtypeStruct((S, H * HEAD), F32)
    return _carried_call(
        body, jobs, "attn_bwd", (H, S // TQ),
        [blk, pl.BlockSpec((S, HEAD), lambda h, i: (0, H + h)),
         pl.BlockSpec((S, HEAD), lambda h, i: (0, 2 * H + h)),
         pl.BlockSpec((None, TQ, 1), lambda h, i: (h, i, 0)), blk],
        [blk, full, full], [shp, shp, shp], [], (zb, zb, zb, ltot, dc), ("arbitrary", "arbitrary"))


def _rg_gates(u, ga, gx, lam):
    r = _sigmoid(ga)
    i = _sigmoid(gx)
    log_a = RG_C * r * _neg_softplus(-lam)
    a = jnp.exp(log_a)
    b = jnp.sqrt(-_expm1(2.0 * log_a)) * (i * u)
    return a, b


def _heads_matmul(u, w_ref, transpose):
    outs = []
    for hh in range(w_ref.shape[0]):
        uh = u[:, hh * HEAD:(hh + 1) * HEAD].astype(BF16)
        wv = w_ref[hh].astype(BF16)
        outs.append(_dot_nt(uh, wv) if transpose else _dot(uh, wv))
    return jnp.concatenate(outs, axis=1)


def _conv(buf, cw_ref, cb_ref, tb):
    u = cb_ref[...] + cw_ref[pl.ds(CONV_WIDTH - 1, 1), :] * buf[pl.ds(8, tb), :]
    for k in range(1, CONV_WIDTH):
        u = u + cw_ref[pl.ds(CONV_WIDTH - 1 - k, 1), :] * buf[pl.ds(8 - k, tb), :]
    return u


def _rnn_specs(tb, C, nb, col_gate, rev):
    blk_of = (lambda i: nb - 1 - i) if rev else (lambda i: i)
    blk = lambda col: pl.BlockSpec((tb, C), lambda i: (blk_of(i), col))
    halo = lambda col: pl.BlockSpec((8, C), lambda i: (jnp.maximum(blk_of(i) * (tb // 8) - 1, 0), col))
    return blk, halo


def rnn_fwd(z, cw, cb, wa, ba, wx, bx, lam):
    S = z.shape[0]
    C = cb.shape[-1]
    tb = _tile(S, 512)
    nb = S // tb

    def body(xg_ref, xr_ref, halo_ref, cw_ref, cb_ref, wa_ref, ba_ref, wx_ref, bx_ref, lam_ref,
             out_ref, hs_ref, buf, a_s, b_s, h_c):
        i = pl.program_id(0)

        @pl.when(i == 0)
        def _():
            h_c[...] = jnp.zeros_like(h_c)

        buf[pl.ds(0, 8), :] = jnp.where(i == 0, 0.0, halo_ref[...])
        buf[pl.ds(8, tb), :] = xr_ref[...]
        u = _conv(buf, cw_ref, cb_ref, tb)
        ga = _heads_matmul(u, wa_ref, False) + ba_ref[...]
        gx = _heads_matmul(u, wx_ref, False) + bx_ref[...]
        a, b = _rg_gates(u, ga, gx, lam_ref[...])
        a_s[...] = a
        b_s[...] = b

        def step(t, h):
            h = a_s[pl.ds(t, 1), :] * h + b_s[pl.ds(t, 1), :]
            hs_ref[pl.ds(t, 1), :] = h
            return h

        h_c[...] = lax.fori_loop(0, tb, step, h_c[...], unroll=8)
        out_ref[...] = (_gelu(xg_ref[...]) * hs_ref[...]).astype(BF16)

    blk, halo = _rnn_specs(tb, C, nb, 6, False)
    full = lambda a: pl.BlockSpec(a.shape, lambda i: (0,) * a.ndim)
    out = pl.BlockSpec((tb, C), lambda i: (i, 0))
    return pl.pallas_call(
        body, name="rnn_fwd", grid=(nb,),
        in_specs=[blk(6), blk(7), halo(7)] + [full(a) for a in (cw, cb, wa, ba, wx, bx, lam)],
        out_specs=[out, out],
        out_shape=[jax.ShapeDtypeStruct((S, C), BF16), jax.ShapeDtypeStruct((S, C), F32)],
        scratch_shapes=[pltpu.VMEM((tb + 8, C), F32), pltpu.VMEM((tb, C), F32), pltpu.VMEM((tb, C), F32),
                        pltpu.VMEM((1, C), F32)],
        compiler_params=_params("arbitrary"))(z, z, z, cw, cb, wa, ba, wx, bx, lam)


def rnn_bwd(z, hs, dc, cw, cb, wa, ba, wx, bx, lam):
    S = z.shape[0]
    C = cb.shape[-1]
    tb = _tile(S, 512)
    nb = S // tb

    def body(xg_ref, xr_ref, halo_ref, hs_ref, hs_halo_ref, do_ref,
             cw_ref, cb_ref, wa_ref, ba_ref, wx_ref, bx_ref, lam_ref,
             dxg_ref, dxr_ref, dcw_ref, dcb_ref, dwa_ref, dba_ref, dwx_ref, dbx_ref, dlam_ref,
             buf, hbuf, dubuf, a_s, dh_s, carry):
        i = pl.program_id(0)
        first = i == nb - 1

        @pl.when(i == 0)
        def _():
            carry[...] = jnp.zeros_like(carry)
            dubuf[pl.ds(tb, 8), :] = jnp.zeros((8, C), F32)
            for r in (dcw_ref, dcb_ref, dwa_ref, dba_ref, dwx_ref, dbx_ref, dlam_ref):
                r[...] = jnp.zeros_like(r)

        buf[pl.ds(0, 8), :] = jnp.where(first, 0.0, halo_ref[...])
        buf[pl.ds(8, tb), :] = xr_ref[...]
        u = _conv(buf, cw_ref, cb_ref, tb)
        ga = _heads_matmul(u, wa_ref, False) + ba_ref[...]
        gx = _heads_matmul(u, wx_ref, False) + bx_ref[...]
        (a, _), gates_vjp = jax.vjp(_rg_gates, u, ga, gx, lam_ref[...])
        gel, gelu_vjp = jax.vjp(_gelu, xg_ref[...])
        hs = hs_ref[...]
        do = do_ref[...]
        dxg_ref[...] = gelu_vjp(do * hs)[0].astype(BF16)
        dh_s[...] = do * gel
        a_s[...] = a

        def step(n, c):
            t = tb - 1 - n
            acc = dh_s[pl.ds(t, 1), :] + c
            dh_s[pl.ds(t, 1), :] = acc
            return a_s[pl.ds(t, 1), :] * acc

        carry[...] = lax.fori_loop(0, tb, step, carry[...], unroll=8)
        db = dh_s[...]
        hbuf[pl.ds(0, 8), :] = jnp.where(first, 0.0, hs_halo_ref[...])
        hbuf[pl.ds(8, tb), :] = hs
        da = db * hbuf[pl.ds(7, tb), :]
        du, dga, dgx, dlam = gates_vjp((da, db))
        du = du + _heads_matmul(dga, wa_ref, True) + _heads_matmul(dgx, wx_ref, True)
        dlam_ref[...] += dlam
        dba_ref[...] += jnp.sum(dga, axis=0, keepdims=True)
        dbx_ref[...] += jnp.sum(dgx, axis=0, keepdims=True)
        ub = u.astype(BF16)
        for hh in range(wa_ref.shape[0]):
            cols = slice(hh * HEAD, (hh + 1) * HEAD)
            dwa_ref[hh] += _dot_tn(ub[:, cols], dga[:, cols].astype(BF16))
            dwx_ref[hh] += _dot_tn(ub[:, cols], dgx[:, cols].astype(BF16))
        dcb_ref[...] += jnp.sum(du, axis=0, keepdims=True)
        dubuf[pl.ds(0, tb), :] = du
        dxr = cw_ref[pl.ds(CONV_WIDTH - 1, 1), :] * du
        dcw_ref[pl.ds(CONV_WIDTH - 1, 1), :] += jnp.sum(du * buf[pl.ds(8, tb), :], axis=0, keepdims=True)
        for k in range(1, CONV_WIDTH):
            dxr = dxr + cw_ref[pl.ds(CONV_WIDTH - 1 - k, 1), :] * dubuf[pl.ds(k, tb), :]
            dcw_ref[pl.ds(CONV_WIDTH - 1 - k, 1), :] += jnp.sum(du * buf[pl.ds(8 - k, tb), :], axis=0,
                                                                keepdims=True)
        dxr_ref[...] = dxr.astype(BF16)
        dubuf[pl.ds(tb, 8), :] = du[0:8, :]

    blk, halo = _rnn_specs(tb, C, nb, 6, True)
    full = lambda a: pl.BlockSpec(a.shape, lambda i: (0,) * a.ndim)
    out = pl.BlockSpec((tb, C), lambda i: (nb - 1 - i, 0))
    params = (cw, cb, wa, ba, wx, bx, lam)
    return pl.pallas_call(
        body, name="rnn_bwd", grid=(nb,),
        in_specs=[blk(6), blk(7), halo(7), blk(0), halo(0), blk(2)] + [full(a) for a in params],
        out_specs=[out, out] + [full(a) for a in params],
        out_shape=[jax.ShapeDtypeStruct((S, C), BF16)] * 2 + [jax.ShapeDtypeStruct(a.shape, F32) for a in params],
        scratch_shapes=[pltpu.VMEM((tb + 8, C), F32), pltpu.VMEM((tb + 8, C), F32), pltpu.VMEM((tb + 8, C), F32),
                        pltpu.VMEM((tb, C), F32), pltpu.VMEM((tb, C), F32), pltpu.VMEM((1, C), F32)],
        compiler_params=_params("arbitrary"))(z, z, z, hs, hs, dc, *params)


def _pool_stats(buf, x, t0, tb, n_groups):
    t = t0 + lax.broadcasted_iota(jnp.int32, (tb, 1), 0)
    ds, cnts = [], []
    for g in range(n_groups):
        win = POOL_WINDOWS[g]
        cols = slice(g * HEAD, (g + 1) * HEAD)
        s = buf[pl.ds(16, tb), cols]
        for k in range(1, win):
            s = s + buf[pl.ds(16 - k, tb), cols]
        cnt = jnp.minimum(t + 1, win).astype(F32)
        ds.append(s / cnt - x[:, cols])
        cnts.append(cnt)
    return ds, cnts


def _pool_specs(tb, C, nb, col, rev):
    blk_of = (lambda i: nb - 1 - i) if rev else (lambda i: i)
    blk = pl.BlockSpec((tb, C), lambda i: (blk_of(i), col))
    halo = pl.BlockSpec((16, C), lambda i: (jnp.maximum(blk_of(i) * (tb // 16) - 1, 0), col))
    return blk, halo


def pool_fwd(z, pw, ps):
    S = z.shape[0]
    C = ps.shape[-1]
    G = pw.shape[0]
    tb = _tile(S, 512)
    nb = S // tb

    def body(x_ref, halo_ref, pw_ref, ps_ref, out_ref, buf):
        i = pl.program_id(0)
        buf[pl.ds(0, 16), :] = jnp.where(i == 0, 0.0, halo_ref[...])
        x = x_ref[...]
        buf[pl.ds(16, tb), :] = x
        ds, _ = _pool_stats(buf, x, i * tb, tb, G)
        y = jnp.concatenate([_dot(ds[g].astype(BF16), pw_ref[g].astype(BF16)) for g in range(G)], axis=1)
        out_ref[...] = (y * ps_ref[...]).astype(BF16)

    blk, halo = _pool_specs(tb, C, nb, 8, False)
    full = lambda a: pl.BlockSpec(a.shape, lambda i: (0,) * a.ndim)
    return pl.pallas_call(
        body, name="pool_fwd", grid=(nb,), in_specs=[blk, halo, full(pw), full(ps)],
        out_specs=pl.BlockSpec((tb, C), lambda i: (i, 0)), out_shape=jax.ShapeDtypeStruct((S, C), BF16),
        scratch_shapes=[pltpu.VMEM((tb + 16, C), F32)], compiler_params=_params("arbitrary"))(z, z, pw, ps)


def pool_bwd(z, dc, pw, ps):
    S = z.shape[0]
    C = ps.shape[-1]
    G = pw.shape[0]
    tb = _tile(S, 512)
    nb = S // tb

    def body(x_ref, halo_ref, dy_ref, pw_ref, ps_ref, dx_ref, dpw_ref, dps_ref, buf, ebuf):
        i = pl.program_id(0)
        ib = nb - 1 - i

        @pl.when(i == 0)
        def _():
            ebuf[pl.ds(tb, 16), :] = jnp.zeros((16, C), F32)
            dpw_ref[...] = jnp.zeros_like(dpw_ref)
            dps_ref[...] = jnp.zeros_like(dps_ref)

        buf[pl.ds(0, 16), :] = jnp.where(ib == 0, 0.0, halo_ref[...])
        x = x_ref[...]
        buf[pl.ds(16, tb), :] = x
        ds, cnts = _pool_stats(buf, x, ib * tb, tb, G)
        dy = dy_ref[...]
        dyp = dy * ps_ref[...]
        ypre, dds, es = [], [], []
        for g in range(G):
            cols = slice(g * HEAD, (g + 1) * HEAD)
            db = ds[g].astype(BF16)
            wv = pw_ref[g].astype(BF16)
            dypb = dyp[:, cols].astype(BF16)
            ypre.append(_dot(db, wv))
            dd = _dot_nt(dypb, wv)
            dpw_ref[g] += _dot_tn(db, dypb)
            dds.append(dd)
            es.append(dd / cnts[g])
        dps_ref[...] += jnp.sum(dy * jnp.concatenate(ypre, axis=1), axis=0, keepdims=True)
        e = jnp.concatenate(es, axis=1)
        ebuf[pl.ds(0, tb), :] = e
        dxs = []
        for g in range(G):
            cols = slice(g * HEAD, (g + 1) * HEAD)
            s = es[g]
            for k in range(1, POOL_WINDOWS[g]):
                s = s + ebuf[pl.ds(k, tb), cols]
            dxs.append(s - dds[g])
        dx_ref[...] = jnp.concatenate(dxs, axis=1).astype(BF16)
        ebuf[pl.ds(tb, 16), :] = e[0:16, :]

    blk, halo = _pool_specs(tb, C, nb, 8, True)
    dyb, _ = _pool_specs(tb, C, nb, 3, True)
    full = lambda a: pl.BlockSpec(a.shape, lambda i: (0,) * a.ndim)
    return pl.pallas_call(
        body, name="pool_bwd", grid=(nb,), in_specs=[blk, halo, dyb, full(pw), full(ps)],
        out_specs=[pl.BlockSpec((tb, C), lambda i: (nb - 1 - i, 0)), full(pw), full(ps)],
        out_shape=[jax.ShapeDtypeStruct((S, C), BF16), jax.ShapeDtypeStruct(pw.shape, F32),
                   jax.ShapeDtypeStruct(ps.shape, F32)],
        scratch_shapes=[pltpu.VMEM((tb + 16, C), F32), pltpu.VMEM((tb + 16, C), F32)],
        compiler_params=_params("arbitrary"))(z, z, dc, pw, ps)


def adamw(w, g, m, v):
    R, N = w.shape
    tr = _tile(R, 256)

    def body(w_ref, g_ref, m_ref, v_ref, d_ref, m2_ref, v2_ref):
        gv = g_ref[...]
        m2 = ADAM_B1 * m_ref[...] + (1.0 - ADAM_B1) * gv
        v2 = ADAM_B2 * v_ref[...] + (1.0 - ADAM_B2) * (gv * gv)
        m_hat = m2 / (1.0 - ADAM_B1 ** ADAM_STEP)
        v_hat = v2 / (1.0 - ADAM_B2 ** ADAM_STEP)
        d_ref[...] = -ADAM_LR * (m_hat / (jnp.sqrt(v_hat) + ADAM_EPS) + ADAM_WD * w_ref[...])
        m2_ref[...] = m2
        v2_ref[...] = v2

    blk = pl.BlockSpec((tr, N), lambda i: (i, 0))
    shp = jax.ShapeDtypeStruct((R, N), F32)
    return pl.pallas_call(body, name="adamw", grid=(R // tr,), in_specs=[blk] * 4, out_specs=[blk] * 3,
                          out_shape=[shp] * 3, compiler_params=_params("arbitrary"))(w, g, m, v)


def own_slot(w, dtype, me_arr):
    L, K, N = w.shape
    tk = _tile(K, 512)

    def body(me_ref, w_ref, o_ref):
        o_ref[...] = w_ref[...].astype(dtype)

    return pl.pallas_call(
        body, name="own_slot",
        grid_spec=pltpu.PrefetchScalarGridSpec(
            num_scalar_prefetch=1, grid=(L, K // tk),
            in_specs=[pl.BlockSpec((None, tk, N), lambda l, i, me: (l, i, 0))],
            out_specs=pl.BlockSpec((None, None, tk, N), lambda l, i, me: (me[0], l, i, 0))),
        out_shape=jax.ShapeDtypeStruct((N_CHIP, L, K, N), dtype),
        compiler_params=_params("arbitrary", "arbitrary"))(me_arr, w)


def chip_sum(g, got, c_arr):
    _, K, N = g.shape
    Kh = K // 2
    tr = _tile(Kh, 352 if Kh % 352 == 0 else 256)

    def body(c_ref, g_ref, got_ref, o_ref):
        o_ref[...] = (g_ref[...] + got_ref[...]).astype(BF16)

    nt = Kh // tr
    return pl.pallas_call(
        body, name="chip_sum",
        grid_spec=pltpu.PrefetchScalarGridSpec(
            num_scalar_prefetch=1, grid=(N_CHIP, nt),
            in_specs=[pl.BlockSpec((None, tr, N), lambda j, i, c: (j, c[0] * nt + i, 0)),
                      pl.BlockSpec((None, tr, N), lambda j, i, c: (j, i, 0))],
            out_specs=pl.BlockSpec((None, tr, N), lambda j, i, c: (j, i, 0))),
        out_shape=jax.ShapeDtypeStruct((N_CHIP, Kh, N), BF16),
        compiler_params=_params("arbitrary", "arbitrary"))(c_arr, g, got)


def total_sum(cs, got, me_arr, c_arr, l, into):
    _, Kh, N = cs.shape
    tr = _tile(Kh, 352 if Kh % 352 == 0 else 256)
    nt = Kh // tr

    def body(me_ref, c_ref, cs_ref, got_ref, *rest):
        o_ref = rest[-1]
        o_ref[...] = ((cs_ref[...].astype(F32) + got_ref[0].astype(F32)) + got_ref[1].astype(F32)) + \
            got_ref[2].astype(F32)

    ins = [pl.BlockSpec((None, tr, N), lambda i, me, c: (me[0], i, 0)),
           pl.BlockSpec((3, tr, N), lambda i, me, c: (0, i, 0))]
    args = [me_arr, c_arr, cs, got]
    alias = {}
    if into is not None:
        ins.append(pl.BlockSpec(memory_space=pl.ANY))
        args.append(into)
        alias = {4: 0}
    return pl.pallas_call(
        body, name="total_sum",
        grid_spec=pltpu.PrefetchScalarGridSpec(
            num_scalar_prefetch=2, grid=(nt,), in_specs=ins,
            out_specs=pl.BlockSpec((None, tr, N), lambda i, me, c: (l, c[0] * nt + i, 0))),
        out_shape=jax.ShapeDtypeStruct((2, 2 * Kh, N), F32), input_output_aliases=alias,
        compiler_params=_params("arbitrary"))(*args)


def join_halves(buf):
    Kh = buf.shape[1] // 2

    def body(in_ref, out_ref, send_sem, recv_sem):
        x, y, c, _ = _place()
        mine = out_ref.at[:, pl.ds(c * Kh, Kh), :]
        cp = pltpu.make_async_remote_copy(
            src_ref=mine, dst_ref=mine, send_sem=send_sem, recv_sem=recv_sem,
            device_id=(x, y, 1 - c), device_id_type=MESH)
        cp.start()
        cp.wait()

    return pl.pallas_call(
        body, name="join_halves", in_specs=[HBM], out_specs=HBM,
        out_shape=jax.ShapeDtypeStruct(buf.shape, F32), input_output_aliases={0: 0},
        scratch_shapes=[pltpu.SemaphoreType.DMA, pltpu.SemaphoreType.DMA])(buf)


def allreduce_small(p):
    R = p.shape[0]

    def body(p_ref, out_ref, all_ref, send_sems, recv_sems, local_sem):
        x, y, c, others = _place()
        me, sibling = (x, y, c), (x, y, 1 - c)

        def rows(px, py, pc):
            return all_ref.at[pl.ds((4 * px + 2 * py + pc) * R, R), :]

        def copy(k, block, to, src=None):
            return pltpu.make_async_remote_copy(
                src_ref=rows(*block) if src is None else src, dst_ref=rows(*block), send_sem=send_sems.at[k],
                recv_sem=recv_sems.at[k], device_id=to, device_id_type=MESH)

        mine = pltpu.make_async_copy(p_ref, rows(*me), local_sem)
        mine.start()
        first = [copy(0, me, sibling, src=p_ref)]
        first += [copy(1 + j, me, (*chip, c), src=p_ref) for j, chip in enumerate(others)]
        for cp in first:
            cp.start()
        passed = [copy(4 + j, (*chip, c), sibling) for j, chip in enumerate(others)]
        for j, chip in enumerate(others):
            copy(1 + j, (*chip, c), me).wait_recv()
            passed[j].start()
        copy(0, sibling, me).wait_recv()
        for j, chip in enumerate(others):
            copy(4 + j, (*chip, 1 - c), me).wait_recv()
        for cp in first + passed:
            cp.wait_send()
        mine.wait()
        total = all_ref[pl.ds(0, R), :]
        for d in range(1, N_DEV):
            total = total + all_ref[pl.ds(d * R, R), :]
        out_ref[...] = total

    vmem = pl.BlockSpec(memory_space=pltpu.VMEM)
    return pl.pallas_call(
        body, name="allreduce_small", in_specs=[vmem], out_specs=vmem,
        out_shape=jax.ShapeDtypeStruct((R, 128), F32),
        scratch_shapes=[pltpu.VMEM((N_DEV * R, 128), F32), pltpu.SemaphoreType.DMA((7,)),
                        pltpu.SemaphoreType.DMA((7,)), pltpu.SemaphoreType.DMA],
        compiler_params=pltpu.CompilerParams(vmem_limit_bytes=VMEM_LIMIT_V7X))(p)


def _ffn_fwd(x, g, Wg, names, l, slots, late):
    h = rmsnorm_fwd(x, g)
    (a, b, y), got = ffn_up(h, Wg[names[0]], Wg[names[1]], l, [_gather_job(slots[n]) for n in late])
    for n, o in zip(late, got):
        Wg[n] = o[0]
    return residual_matmul(x, y, Wg[names[2]], l, 0.5, True), (x, h, a, b, y)


class _Reducer:
    def __init__(self, c_arr):
        self.c_arr, self.fresh, self.summed, self.landed = c_arr, [], [], {}

    def jobs(self):
        return [_swap_job(g) for _, _, g in self.fresh] + [_scatter_job(cs) for _, _, cs in self.summed]

    def done(self, got):
        n_swap = len(self.fresh)
        for (n, l, cs), o in zip(self.summed, got[n_swap:]):
            self.landed[n, l] = (cs, o[0])
        self.summed = [(n, l, chip_sum(g, o[0], self.c_arr)) for (n, l, g), o in zip(self.fresh, got[:n_swap])]
        self.fresh = []

    def weight_grad(self, name, l, *args, **kw):
        g, got = weight_grad(*args, jobs=self.jobs(), **kw)
        self.done(got)
        self.fresh.append((name, l, g))

    def flush(self):
        while self.fresh or self.summed:
            self.done(_run_jobs(self.jobs(), "reduce_tail"))


def _ffn_bwd(red, names, dxo, saved, g, Wg, l):
    x, h, a, b, y = saved
    D = x.shape[1]
    wg, wu, wd = (Wg[n] for n in names)
    F = wg.shape[-1]
    da, db = ffn_bwd_hidden(dxo, wd, a, b, l)
    red.weight_grad(names[2], l, y, dxo, F, D, True, False, b_scale=0.5)
    red.weight_grad(names[0], l, h, da, D, F, False, True)
    red.weight_grad(names[1], l, h, db, D, F, False, True)
    (dx, dg), got = norm_bwd_matmul(x, g, dxo, [(da, wg), (db, wu)], l, True, red.jobs())
    red.done(got)
    return dx, dg


def kernel(x, norm_ffn1, ffn1_gate, ffn1_up, ffn1_down, norm_mix, w_in, conv_w, conv_b, rg_w_a, rg_b_a, rg_w_x, rg_b_x, rg_lambda, pool_w, pool_scale, w_out, norm_ffn2, ffn2_gate, ffn2_up, ffn2_down, norm_final, loss_target, m_norm_ffn1, m_ffn1_gate, m_ffn1_up, m_ffn1_down, m_norm_mix, m_w_in, m_conv_w, m_conv_b, m_rg_w_a, m_rg_b_a, m_rg_w_x, m_rg_b_x, m_rg_lambda, m_pool_w, m_pool_scale, m_w_out, m_norm_ffn2, m_ffn2_gate, m_ffn2_up, m_ffn2_down, m_norm_final, v_norm_ffn1, v_ffn1_gate, v_ffn1_up, v_ffn1_down, v_norm_mix, v_w_in, v_conv_w, v_conv_b, v_rg_w_a, v_rg_b_a, v_rg_w_x, v_rg_b_x, v_rg_lambda, v_pool_w, v_pool_scale, v_w_out, v_norm_ffn2, v_ffn2_gate, v_ffn2_up, v_ffn2_down, v_norm_final):
    given = dict(locals())
    W = {n: given[n] for n in WEIGHTS}
    M = {n: given["m_" + n] for n in WEIGHTS}
    V = {n: given["v_" + n] for n in WEIGHTS}
    depth = norm_ffn1.shape[0]
    S, D = x.shape[1], x.shape[2]
    xs = x.reshape(S, D)
    target = loss_target.reshape(S, D)
    C = conv_b.shape[-1]
    n_heads = (D // 2) // HEAD
    cx, cy, cc = lax.axis_index("x"), lax.axis_index("y"), lax.axis_index("c")
    me = 2 * cx + cy
    me_arr = jnp.reshape(me, (1,)).astype(jnp.int32)
    c_arr = jnp.reshape(cc, (1,)).astype(jnp.int32)

    slots = {n: own_slot(W[n], BF16, me_arr) for n in BIG}
    Wg = {n: _run_jobs([_gather_job(slots[n])], "gather_weight")[0][0] for n in ('ffn1_gate', 'ffn1_up')}
    conv_all = _run_jobs([_gather_job(own_slot(conv_w, F32, me_arr))], "gather_weight")[0][0]
    conv_full = jnp.transpose(conv_all, (1, 2, 0, 3)).reshape(depth, CONV_WIDTH, C)

    row = lambda a, l: a[l].reshape(1, -1)

    saved = []
    cur = xs
    for l in range(depth):
        cur, s1 = _ffn_fwd(cur, row(norm_ffn1, l), Wg, ('ffn1_gate', 'ffn1_up', 'ffn1_down'), l, slots,
                           ['ffn1_down', 'w_in'] if l == 0 else [])
        x1 = cur
        hm = rmsnorm_fwd(x1, row(norm_mix, l))
        z, zb = mix_in(hm, Wg['w_in'], l)
        late = ['w_out', 'ffn2_gate', 'ffn2_up', 'ffn2_down'] if l == 0 else []
        (att, ltot), got = attn_fwd(zb, n_heads, [_gather_job(slots[n]) for n in late])
        for n, o in zip(late, got):
            Wg[n] = o[0]
        rnn_p = (conv_full[l], row(conv_b, l), rg_w_a[l], row(rg_b_a, l), rg_w_x[l], row(rg_b_x, l),
                 row(rg_lambda, l))
        rnn, hs = rnn_fwd(z, *rnn_p)
        pool = pool_fwd(z, pool_w[l], row(pool_scale, l))
        cat = jnp.concatenate([att, rnn, pool], axis=1)
        cur = residual_matmul(x1, cat, Wg['w_out'], l, 1.0, False)
        cur, s2 = _ffn_fwd(cur, row(norm_ffn2, l), Wg, ('ffn2_gate', 'ffn2_up', 'ffn2_down'), l, slots, [])
        saved.append((s1, (x1, hm, z, zb, ltot, hs, cat, rnn_p), s2))

    loss_cols, dx, g_norm_final = final_loss(cur, norm_final.reshape(1, D), target)

    G = {n: [None] * depth for n in SMALL if n != 'norm_final'}
    red = _Reducer(c_arr)
    for l in reversed(range(depth)):
        s1, (x1, hm, z, zb, ltot, hs, cat, rnn_p), s2 = saved[l]
        dx, G['norm_ffn2'][l] = _ffn_bwd(red, ('ffn2_gate', 'ffn2_up', 'ffn2_down'), dx, s2, row(norm_ffn2, l), Wg, l)
        dc = mix_out_bwd(dx, Wg['w_out'], l)
        red.weight_grad('w_out', l, cat, dx, C, D, False, False)
        (dq, dk, dv), got = attn_bwd(zb, ltot, dc, n_heads, red.jobs())
        red.done(got)
        (dxg, dxr, G['conv_w'][l], G['conv_b'][l], G['rg_w_a'][l], G['rg_b_a'][l], G['rg_w_x'][l],
         G['rg_b_x'][l], G['rg_lambda'][l]) = rnn_bwd(z, hs, dc, *rnn_p)
        dxp, G['pool_w'][l], G['pool_scale'][l] = pool_bwd(z, dc, pool_w[l], row(pool_scale, l))
        dz = jnp.concatenate([dq.astype(BF16), dk.astype(BF16), dv.astype(BF16), dxg, dxr, dxp], axis=1)
        red.weight_grad('w_in', l, hm, dz, D, dz.shape[1] // N_CHIP, False, False)
        (dx, G['norm_mix'][l]), got = norm_bwd_matmul(x1, row(norm_mix, l), dx, [(dz, Wg['w_in'])], l, False,
                                                      red.jobs())
        red.done(got)
        dx, G['norm_ffn1'][l] = _ffn_bwd(red, ('ffn1_gate', 'ffn1_up', 'ffn1_down'), dx, s1, row(norm_ffn1, l), Wg, l)
    grad_x = dx.reshape(x.shape)
    red.flush()

    outs = {}
    for n in BIG:
        halves = None
        for l in range(depth):
            cs, got = red.landed[n, l]
            halves = total_sum(cs, got, me_arr, c_arr, l, halves)
        g_red = join_halves(halves)
        two_d = (-1, g_red.shape[-1])
        d, m2, v2 = adamw(W[n].reshape(two_d), g_red.reshape(two_d), M[n].reshape(two_d), V[n].reshape(two_d))
        outs[n] = (g_red, d.reshape(W[n].shape), m2.reshape(W[n].shape), v2.reshape(W[n].shape))

    loss_part = 0.5 * jnp.sum(loss_cols) / D
    parts = [jnp.stack(G[n]).reshape(-1) for n in SMALL if n != 'norm_final']
    parts += [g_norm_final.reshape(-1), jnp.reshape(loss_part, (1,))]
    sizes = [p.shape[0] for p in parts]
    total = sum(sizes)
    R = -(-total // 1024) * 8
    packed = jnp.concatenate(parts + [jnp.zeros((R * 128 - total,), F32)]).reshape(R, 128)
    red = allreduce_small(packed).reshape(-1)
    offs = [sum(sizes[:i]) for i in range(len(sizes))]
    small_names = [n for n in SMALL if n != 'norm_final'] + ['norm_final']
    small_g = {}
    for n, o, sz in zip(small_names, offs, sizes):
        if n == 'conv_w':
            full = red[o:o + sz].reshape(depth, CONV_WIDTH, C)
            small_g[n] = lax.dynamic_slice_in_dim(full, me * (C // N_CHIP), C // N_CHIP, axis=2)
        else:
            small_g[n] = red[o:o + sz].reshape(W[n].shape)
    loss = red[offs[-1]]

    def pack(d):
        flat = jnp.concatenate([d[n].reshape(-1) for n in small_names])
        rows = -(-flat.shape[0] // 1024) * 8
        return jnp.concatenate([flat, jnp.ones((rows * 128 - flat.shape[0],), F32)]).reshape(rows, 128)

    d_s, m_s, v_s = adamw(pack(W), pack(small_g), pack(M), pack(V))
    o = 0
    for n in small_names:
        sz = W[n].size
        outs[n] = (small_g[n],) + tuple(a.reshape(-1)[o:o + sz].reshape(W[n].shape) for a in (d_s, m_s, v_s))
        o += sz

    return (loss, grad_x, *[outs[n][0] for n in WEIGHTS], *[outs[n][1] for n in WEIGHTS],
            *[outs[n][2] for n in WEIGHTS], *[outs[n][3] for n in WEIGHTS])
```

```python
import functools

import jax
import jax.numpy as jnp
from jax import lax
from jax.experimental import pallas as pl
from jax.experimental.pallas import tpu as pltpu

F32 = jnp.float32
BF16 = jnp.bfloat16
MESH = pl.DeviceIdType.MESH

HEAD = 128
ATTN_TQ, ATTN_TK = 1024, 256
ATTN_TQ_FWD = 2048
POOL_WINDOWS = (2, 4, 8, 16)
CONV_WIDTH = 4
RG_C = 8.0
NORM_EPS = 1e-6
N_CHIP = 4
N_DEV = 8
ADAM_LR, ADAM_B1, ADAM_B2, ADAM_EPS, ADAM_WD, ADAM_STEP = 0.001, 0.9, 0.999, 1e-08, 0.01, 10
VMEM_LIMIT_V7X = 56 * 2 ** 20

WEIGHTS = ['norm_ffn1', 'ffn1_gate', 'ffn1_up', 'ffn1_down', 'norm_mix', 'w_in', 'conv_w', 'conv_b', 'rg_w_a',
           'rg_b_a', 'rg_w_x', 'rg_b_x', 'rg_lambda', 'pool_w', 'pool_scale', 'w_out', 'norm_ffn2', 'ffn2_gate',
           'ffn2_up', 'ffn2_down', 'norm_final']
BIG = ['ffn1_gate', 'ffn1_up', 'ffn1_down', 'w_in', 'w_out', 'ffn2_gate', 'ffn2_up', 'ffn2_down']
SMALL = [n for n in WEIGHTS if n not in BIG]


def _params(*sem, vmem=None):
    return pltpu.CompilerParams(dimension_semantics=sem, vmem_limit_bytes=vmem or VMEM_LIMIT_V7X)


def _tile(n, pref):
    t = min(n, pref)
    while n % t:
        t //= 2
    return t


def _dot(a, b):
    return jnp.dot(a, b, preferred_element_type=F32)


def _dot_nt(a, b):
    return lax.dot_general(a, b, (((1,), (1,)), ((), ())), preferred_element_type=F32)


def _dot_tn(a, b):
    return lax.dot_general(a, b, (((0,), (0,)), ((), ())), preferred_element_type=F32)


def _sigmoid(x):
    return 1.0 / (1.0 + jnp.exp(-x))


def _gelu(x):
    return 0.5 * x * (1.0 + jnp.tanh(0.7978845608028654 * (x + 0.044715 * x * x * x)))


def _expm1(x):
    series = x * (1.0 + x * 0.5 * (1.0 + x * (1.0 / 3.0) * (1.0 + x * 0.25 * (1.0 + x * 0.2))))
    return jnp.where(jnp.abs(x) < 0.1, series, jnp.exp(x) - 1.0)


def _neg_softplus(z):
    return -(jnp.maximum(z, 0.0) + jnp.log(1.0 + jnp.exp(-jnp.abs(z))))


def _wspec(k, n, l, jpos):
    return pl.BlockSpec((None, None, k, n), lambda *g: (g[jpos], l, 0, 0))


HBM = pl.BlockSpec(memory_space=pltpu.HBM)


def _place():
    x, y, c = lax.axis_index("x"), lax.axis_index("y"), lax.axis_index("c")
    others = [(1 - x, y), (x, 1 - y), (1 - x, 1 - y)]
    return x, y, c, others


def _scatter_job(cs):
    def copies(ins, outs, sems):
        x, y, c, others = _place()
        return [pltpu.make_async_remote_copy(
            src_ref=ins[0].at[2 * chip[0] + chip[1]], dst_ref=outs[0].at[k], send_sem=sems[0].at[k],
            recv_sem=sems[1].at[k], device_id=(*chip, c), device_id_type=MESH) for k, chip in enumerate(others)]

    def start(ins, outs, sems):
        for cp in copies(ins, outs, sems):
            cp.start()

    def finish(ins, outs, sems):
        for cp in copies(ins, outs, sems):
            cp.wait()

    return dict(ins=[cs], outs=[jax.ShapeDtypeStruct((3,) + cs.shape[1:], cs.dtype)], alias={},
                sems=[pltpu.SemaphoreType.DMA((3,)), pltpu.SemaphoreType.DMA((3,))], start=start, finish=finish)


def _swap_job(g):
    Kh = g.shape[1] // 2

    def copy(ins, outs, sems):
        x, y, c, _ = _place()
        return pltpu.make_async_remote_copy(
            src_ref=ins[0].at[:, pl.ds((1 - c) * Kh, Kh), :], dst_ref=outs[0], send_sem=sems[0], recv_sem=sems[1],
            device_id=(x, y, 1 - c), device_id_type=MESH)

    return dict(ins=[g], outs=[jax.ShapeDtypeStruct((N_CHIP, Kh, g.shape[2]), g.dtype)], alias={},
                sems=[pltpu.SemaphoreType.DMA, pltpu.SemaphoreType.DMA],
                start=lambda ins, outs, sems: copy(ins, outs, sems).start(),
                finish=lambda ins, outs, sems: copy(ins, outs, sems).wait())


def _gather_job(buf):
    def copy(outs, sems, k, chip, layer, to):
        dst = outs[0].at[2 * chip[0] + chip[1], layer]
        return pltpu.make_async_remote_copy(src_ref=dst, dst_ref=dst, send_sem=sems[0].at[k], recv_sem=sems[1].at[k],
                                            device_id=to, device_id_type=MESH)

    def start(ins, outs, sems):
        x, y, c, others = _place()
        for k, chip in enumerate(others):
            copy(outs, sems, k, (x, y), c, (*chip, c)).start()

    def finish(ins, outs, sems):
        x, y, c, others = _place()
        me, sibling = (x, y, c), (x, y, 1 - c)
        for k, chip in enumerate(others):
            copy(outs, sems, k, chip, c, me).wait_recv()
            copy(outs, sems, 3 + k, chip, c, sibling).start()
        for k, chip in enumerate(others):
            copy(outs, sems, 3 + k, chip, 1 - c, me).wait_recv()
        for k in range(6):
            copy(outs, sems, k, (x, y), c, me).wait_send()

    return dict(ins=[buf], outs=[jax.ShapeDtypeStruct(buf.shape, buf.dtype)], alias={0: 0},
                sems=[pltpu.SemaphoreType.DMA((6,)), pltpu.SemaphoreType.DMA((6,))], start=start, finish=finish)


def _carried_call(body, jobs, name, grid, in_specs, out_specs, out_shape, scratch_shapes, args, sem, vmem=None):
    n_in, n_out, n_scr = len(in_specs), len(out_specs), len(scratch_shapes)
    j_in = [a for j in jobs for a in j['ins']]
    j_out = [o for j in jobs for o in j['outs']]
    j_sem = [s for j in jobs for s in j['sems']]

    def wrapped(*refs):
        own_in, ji = refs[:n_in], list(refs[n_in:n_in + len(j_in)])
        rest = refs[n_in + len(j_in):]
        own_out, jo = rest[:n_out], list(rest[n_out:n_out + len(j_out)])
        rest = rest[n_out + len(j_out):]
        own_scr, js = rest[:n_scr], list(rest[n_scr:])
        per_job = []
        for j in jobs:
            per_job.append((j, [ji.pop(0) for _ in j['ins']], [jo.pop(0) for _ in j['outs']],
                            [js.pop(0) for _ in j['sems']]))
        ids = [pl.program_id(d) for d in range(len(grid))]
        first = functools.reduce(lambda a, b: a & b, [i == 0 for i in ids])
        last = functools.reduce(lambda a, b: a & b, [i == g - 1 for i, g in zip(ids, grid)])

        if jobs:
            @pl.when(first)
            def _():
                for j, a, o, s in per_job:
                    j['start'](a, o, s)

        body(*own_in, *own_out, *own_scr)

        if jobs:
            @pl.when(last)
            def _():
                for j, a, o, s in per_job:
                    j['finish'](a, o, s)

    alias, off_in, off_out = {}, n_in, n_out
    for j in jobs:
        for li, lo in j['alias'].items():
            alias[off_in + li] = off_out + lo
        off_in += len(j['ins'])
        off_out += len(j['outs'])
    res = pl.pallas_call(
        wrapped, name=name, grid=grid, in_specs=list(in_specs) + [HBM] * len(j_in),
        out_specs=list(out_specs) + [HBM] * len(j_out), out_shape=list(out_shape) + j_out,
        scratch_shapes=list(scratch_shapes) + j_sem, input_output_aliases=alias,
        compiler_params=_params(*sem, vmem=vmem))(*args, *j_in)
    own, rest = list(res[:n_out]), list(res[n_out:])
    return own, [[rest.pop(0) for _ in j['outs']] for j in jobs]


def _run_jobs(jobs, name):
    def body(*refs):
        ins, rest = list(refs[:n_in]), list(refs[n_in:])
        outs, sems = rest[:n_out], rest[n_out:]
        per_job = [(j, [ins.pop(0) for _ in j['ins']], [outs.pop(0) for _ in j['outs']],
                    [sems.pop(0) for _ in j['sems']]) for j in jobs]
        for j, a, o, s in per_job:
            j['start'](a, o, s)
        for j, a, o, s in per_job:
            j['finish'](a, o, s)

    j_in = [a for j in jobs for a in j['ins']]
    j_out = [o for j in jobs for o in j['outs']]
    n_in, n_out = len(j_in), len(j_out)
    alias, off_in, off_out = {}, 0, 0
    for j in jobs:
        for li, lo in j['alias'].items():
            alias[off_in + li] = off_out + lo
        off_in += len(j['ins'])
        off_out += len(j['outs'])
    res = list(pl.pallas_call(
        body, name=name, in_specs=[HBM] * n_in, out_specs=[HBM] * n_out, out_shape=j_out,
        scratch_shapes=[s for j in jobs for s in j['sems']], input_output_aliases=alias)(*j_in))
    return [[res.pop(0) for _ in j['outs']] for j in jobs]


def rmsnorm_fwd(x, g):
    S, D = x.shape
    tm = _tile(S, 512)

    def body(x_ref, g_ref, h_ref):
        xv = x_ref[...]
        rstd = lax.rsqrt(jnp.mean(xv * xv, axis=-1, keepdims=True) + NORM_EPS)
        h_ref[...] = (xv * rstd * g_ref[...]).astype(BF16)

    return pl.pallas_call(
        body, name="rmsnorm_fwd", grid=(S // tm,),
        in_specs=[pl.BlockSpec((tm, D), lambda i: (i, 0)), pl.BlockSpec((1, D), lambda i: (0, 0))],
        out_specs=pl.BlockSpec((tm, D), lambda i: (i, 0)),
        out_shape=jax.ShapeDtypeStruct((S, D), BF16), compiler_params=_params("arbitrary"))(x, g)


def ffn_up(h, wg, wu, l, jobs=()):
    S, D = h.shape
    F = wg.shape[-1]
    tm = _tile(S, 512)

    def body(h_ref, wg_ref, wu_ref, a_ref, b_ref, y_ref):
        hv = h_ref[...]
        a = _dot(hv, wg_ref[...])
        b = _dot(hv, wu_ref[...])
        a_ref[...] = a.astype(BF16)
        b_ref[...] = b.astype(BF16)
        y_ref[...] = (a * _sigmoid(a) * b).astype(BF16)

    out = pl.BlockSpec((None, tm, F), lambda j, i: (j, i, 0))
    shp = jax.ShapeDtypeStruct((N_CHIP, S, F), BF16)
    return _carried_call(
        body, jobs, "ffn_up", (N_CHIP, S // tm),
        [pl.BlockSpec((tm, D), lambda j, i: (i, 0)), _wspec(D, F, l, 0), _wspec(D, F, l, 0)],
        [out, out, out], [shp, shp, shp], [], (h, wg, wu), ("arbitrary", "arbitrary"))


def residual_matmul(x, a, w, l, scale, a_blocked):
    S, D = x.shape
    Kb = w.shape[2]
    tm = _tile(S, 512)
    dc = _tile(D, 512)

    def body(x_ref, a_ref, w_ref, o_ref):
        a_blocks = [a_ref[j] if a_blocked else a_ref[:, pl.ds(j * Kb, Kb)] for j in range(N_CHIP)]
        for c0 in range(0, D, dc):
            cols = pl.ds(c0, dc)
            tot = _dot(a_blocks[0], w_ref[0, :, cols])
            for j in range(1, N_CHIP):
                tot = tot + _dot(a_blocks[j], w_ref[j, :, cols])
            o_ref[:, cols] = x_ref[:, cols] + scale * tot

    a_spec = (pl.BlockSpec((N_CHIP, tm, Kb), lambda i: (0, i, 0)) if a_blocked
              else pl.BlockSpec((tm, N_CHIP * Kb), lambda i: (i, 0)))
    w_spec = pl.BlockSpec((N_CHIP, None, Kb, D), lambda i: (0, l, 0, 0), pipeline_mode=pl.Buffered(1))
    return pl.pallas_call(
        body, name="residual_matmul", grid=(S // tm,),
        in_specs=[pl.BlockSpec((tm, D), lambda i: (i, 0)), a_spec, w_spec],
        out_specs=pl.BlockSpec((tm, D), lambda i: (i, 0)),
        out_shape=jax.ShapeDtypeStruct((S, D), F32),
        compiler_params=_params("arbitrary"))(x, a, w)


def norm_bwd_matmul(x, g, dxo, pairs, l, g_blocked, jobs=()):
    S, D = x.shape
    Nb = pairs[0][1].shape[-1]
    dc = _tile(D, 512)
    tc = 128

    def products(g_ref, w_ref, prev_ref, o_ref):
        g_blocks = [g_ref[j] if g_blocked else g_ref[:, pl.ds(j * Nb, Nb)] for j in range(N_CHIP)]
        for c0 in range(0, D, dc):
            cols = pl.ds(c0, dc)
            tot = _dot_nt(g_blocks[0], w_ref[0, cols, :])
            for j in range(1, N_CHIP):
                tot = tot + _dot_nt(g_blocks[j], w_ref[j, cols, :])
            o_ref[:, cols] = tot if prev_ref is None else tot + prev_ref[:, cols]

    def specs(tm):
        row = pl.BlockSpec((tm, D), lambda i: (i, 0))
        g_spec = (pl.BlockSpec((N_CHIP, tm, Nb), lambda i: (0, i, 0)) if g_blocked
                  else pl.BlockSpec((tm, N_CHIP * Nb), lambda i: (i, 0)))
        w_spec = pl.BlockSpec((N_CHIP, None, D, Nb), lambda i: (0, l, 0, 0), pipeline_mode=pl.Buffered(1))
        return row, g_spec, w_spec

    prev = None
    for G, Wp in pairs[:-1]:
        tm = _tile(S, 512)
        row, g_spec, w_spec = specs(tm)
        has_prev = prev is not None

        def part_body(*refs):
            products(refs[0], refs[1], refs[2] if has_prev else None, refs[-1])

        prev = pl.pallas_call(
            part_body, name="nt_matmul_sum", grid=(S // tm,),
            in_specs=[g_spec, w_spec] + ([row] if has_prev else []), out_specs=row,
            out_shape=jax.ShapeDtypeStruct((S, D), F32),
            compiler_params=_params("arbitrary"))(G, Wp, *([prev] if has_prev else []))

    tm = _tile(S, 256)
    row, g_spec, w_spec = specs(tm)
    vec = pl.BlockSpec((1, D), lambda i: (0, 0))
    has_prev = prev is not None

    def body(*refs):
        x_ref, g_ref, dxo_ref, gg_ref, w_ref = refs[:5]
        dx_ref, dg_ref = refs[-2:]

        @pl.when(pl.program_id(0) == 0)
        def _():
            dg_ref[...] = jnp.zeros_like(dg_ref)

        products(gg_ref, w_ref, refs[5] if has_prev else None, dx_ref)
        dg = jnp.zeros((1, D), F32)
        for r in range(0, tm, tc):
            rows = pl.ds(r, min(tc, tm))
            xv = x_ref[rows, :]
            rstd = lax.rsqrt(jnp.mean(xv * xv, axis=-1, keepdims=True) + NORM_EPS)
            xhat = xv * rstd
            dh = dx_ref[rows, :]
            gd = dh * g_ref[...]
            dx_ref[rows, :] = dxo_ref[rows, :] + rstd * (gd - xhat * jnp.mean(gd * xhat, axis=-1, keepdims=True))
            dg = dg + jnp.sum(dh * xhat, axis=0, keepdims=True)
        dg_ref[...] += dg

    G, Wp = pairs[-1]
    return _carried_call(
        body, jobs, "norm_bwd_matmul", (S // tm,),
        [row, vec, row, g_spec, w_spec] + ([row] if has_prev else []), [row, vec],
        [jax.ShapeDtypeStruct((S, D), F32), jax.ShapeDtypeStruct((1, D), F32)], [],
        (x, g, dxo, G, Wp, *([prev] if has_prev else [])), ("arbitrary",))


def ffn_bwd_hidden(dxo, wd, a, b, l):
    S, D = dxo.shape
    F = wd.shape[2]
    tm = _tile(S, 512)

    def body(dxo_ref, wd_ref, a_ref, b_ref, da_ref, db_ref):
        dy = 0.5 * _dot_nt(dxo_ref[...].astype(BF16), wd_ref[...])
        av = a_ref[...].astype(F32)
        bv = b_ref[...].astype(F32)
        s = _sigmoid(av)
        da_ref[...] = (dy * bv * (s * (1.0 + av * (1.0 - s)))).astype(BF16)
        db_ref[...] = (dy * (av * s)).astype(BF16)

    blk = pl.BlockSpec((None, tm, F), lambda j, i: (j, i, 0))
    shp = jax.ShapeDtypeStruct((N_CHIP, S, F), BF16)
    return pl.pallas_call(
        body, name="ffn_bwd_hidden", grid=(N_CHIP, S // tm),
        in_specs=[pl.BlockSpec((tm, D), lambda j, i: (i, 0)), _wspec(F, D, l, 0), blk, blk],
        out_specs=[blk, blk], out_shape=[shp, shp],
        compiler_params=_params("arbitrary", "arbitrary"))(dxo, wd, a, b)


def weight_grad(a, b, m, n, a_blocked, b_blocked, b_scale=1.0, jobs=()):
    S = a.shape[-2]
    ts = _tile(S, 1024)
    tm = m if m % 256 else _tile(m, 1024)
    tn = n if n % 256 else _tile(n, 1024)
    if m * n * 4 <= 6 * 2 ** 20:
        tm, tn = m, n

    def spec(arr, blocked, width, t, pos):
        if blocked:
            return pl.BlockSpec((None, ts, t), lambda j, mi, ni, si: (j, si, (mi, ni)[pos]))
        if arr.shape[-1] == width:
            return pl.BlockSpec((ts, t), lambda j, mi, ni, si: (si, (mi, ni)[pos]))
        nblk = width // t
        return pl.BlockSpec((ts, t), lambda j, mi, ni, si: (si, j * nblk + (mi, ni)[pos]))

    def body(a_ref, b_ref, o_ref):
        @pl.when(pl.program_id(3) == 0)
        def _():
            o_ref[...] = jnp.zeros_like(o_ref)

        bv = b_ref[...]
        if b_scale != 1.0:
            bv = b_scale * bv
        o_ref[...] += _dot_tn(a_ref[...], bv.astype(BF16))

    (out,), got = _carried_call(
        body, jobs, "weight_grad", (N_CHIP, m // tm, n // tn, S // ts),
        [spec(a, a_blocked, m, tm, 0), spec(b, b_blocked, n, tn, 1)],
        [pl.BlockSpec((None, tm, tn), lambda j, mi, ni, si: (j, mi, ni))],
        [jax.ShapeDtypeStruct((N_CHIP, m, n), F32)], [], (a, b),
        ("arbitrary", "arbitrary", "arbitrary", "arbitrary"))
    return out, got


def mix_in(h, w_in, l):
    S, D = h.shape
    Nb = w_in.shape[-1]
    tm = _tile(S, 512)

    def body(h_ref, w_ref, z_ref, zb_ref):
        z = _dot(h_ref[...], w_ref[...])
        z_ref[...] = z
        zb_ref[...] = z.astype(BF16)

    out = pl.BlockSpec((tm, Nb), lambda j, i: (i, j))
    return pl.pallas_call(
        body, name="mix_in", grid=(N_CHIP, S // tm),
        in_specs=[pl.BlockSpec((tm, D), lambda j, i: (i, 0)), _wspec(D, Nb, l, 0)],
        out_specs=[out, out],
        out_shape=[jax.ShapeDtypeStruct((S, N_CHIP * Nb), F32), jax.ShapeDtypeStruct((S, N_CHIP * Nb), BF16)],
        compiler_params=_params("arbitrary", "arbitrary"))(h, w_in)


def mix_out_bwd(dx, w_out, l):
    S, D = dx.shape
    Kb = w_out.shape[2]
    tm = _tile(S, 512)

    def body(dx_ref, w_ref, dc_ref):
        dc_ref[...] = _dot_nt(dx_ref[...].astype(BF16), w_ref[...])

    return pl.pallas_call(
        body, name="mix_out_bwd", grid=(N_CHIP, S // tm),
        in_specs=[pl.BlockSpec((tm, D), lambda j, i: (i, 0)), _wspec(Kb, D, l, 0)],
        out_specs=pl.BlockSpec((tm, Kb), lambda j, i: (i, j)),
        out_shape=jax.ShapeDtypeStruct((S, N_CHIP * Kb), F32),
        compiler_params=_params("arbitrary", "arbitrary"))(dx, w_out)


def final_loss(x, g, target):
    S, D = x.shape
    tm = _tile(S, 256)

    def body(x_ref, g_ref, t_ref, loss_ref, dx_ref, dg_ref):
        @pl.when(pl.program_id(0) == 0)
        def _():
            loss_ref[...] = jnp.zeros_like(loss_ref)
            dg_ref[...] = jnp.zeros_like(dg_ref)

        xv = x_ref[...]
        rstd = lax.rsqrt(jnp.mean(xv * xv, axis=-1, keepdims=True) + NORM_EPS)
        xhat = xv * rstd
        err = xhat * g_ref[...] - t_ref[...]
        loss_ref[...] += jnp.sum(err * err, axis=0, keepdims=True)
        dy = err * (1.0 / D)
        gd = dy * g_ref[...]
        dx_ref[...] = rstd * (gd - xhat * jnp.mean(gd * xhat, axis=-1, keepdims=True))
        dg_ref[...] += jnp.sum(dy * xhat, axis=0, keepdims=True)

    row = pl.BlockSpec((tm, D), lambda i: (i, 0))
    vec = pl.BlockSpec((1, D), lambda i: (0, 0))
    return pl.pallas_call(
        body, name="final_loss", grid=(S // tm,), in_specs=[row, vec, row], out_specs=[vec, row, vec],
        out_shape=[jax.ShapeDtypeStruct((1, D), F32), jax.ShapeDtypeStruct((S, D), F32),
                   jax.ShapeDtypeStruct((1, D), F32)],
        compiler_params=_params("arbitrary"))(x, g, target)


def _tri(T, strict_lower):
    row = lax.broadcasted_iota(jnp.int32, (T, T), 0)
    col = lax.broadcasted_iota(jnp.int32, (T, T), 1)
    return jnp.where((row > col) if strict_lower else (row < col), 1.0, 0.0).astype(BF16)


def _causal(rows, cols):
    return lax.broadcasted_iota(jnp.int32, (rows, cols), 1) < lax.broadcasted_iota(jnp.int32, (rows, cols), 0)


def attn_fwd(zb, n_heads, jobs=()):
    S = zb.shape[0]
    TQ = _tile(S, ATTN_TQ_FWD)
    TK = min(ATTN_TK, TQ)
    R = TQ // TK
    scale = HEAD ** -0.5

    def body(q_ref, k_ref, v_ref, o_ref, l_ref):
        i = pl.program_id(1)
        q = q_ref[...]
        later = _tri(TK, True)

        def tile(qr, k, v, carry, acc, mask):
            z = _dot_nt(qr, k) * scale
            lf = _neg_softplus(z)
            if mask is not None:
                lf = jnp.where(mask, lf, 0.0)
            after = _dot(lf.astype(BF16), later)
            w = jnp.exp(z + lf + after + carry)
            if mask is not None:
                w = jnp.where(mask, w, 0.0)
            acc = acc + _dot(w.astype(BF16), v)
            carry = carry + jnp.sum(lf, axis=1, keepdims=True)
            return carry, acc

        def kv(kb):
            start = pl.multiple_of(kb * TK, TK)
            return k_ref[pl.ds(start, TK), :], v_ref[pl.ds(start, TK), :]

        carry, acc = jnp.zeros((TQ, 1), F32), jnp.zeros((TQ, HEAD), F32)
        for r in reversed(range(R)):
            k, v = kv(i * R + r)
            c_r, a_r = tile(q[r * TK:], k, v, carry[r * TK:], acc[r * TK:], _causal(TQ - r * TK, TK))
            carry = c_r if r == 0 else jnp.concatenate([carry[:r * TK], c_r], axis=0)
            acc = a_r if r == 0 else jnp.concatenate([acc[:r * TK], a_r], axis=0)
        carry, acc = lax.fori_loop(0, i * R, lambda n, ca: tile(q, *kv(i * R - 1 - n), ca[0], ca[1], None),
                                   (carry, acc))
        o_ref[...] = acc.astype(BF16)
        l_ref[...] = carry

    H = n_heads
    return _carried_call(
        body, jobs, "attn_fwd", (H, S // TQ),
        [pl.BlockSpec((TQ, HEAD), lambda h, i: (i, h)),
         pl.BlockSpec((S, HEAD), lambda h, i: (0, H + h)),
         pl.BlockSpec((S, HEAD), lambda h, i: (0, 2 * H + h))],
        [pl.BlockSpec((TQ, HEAD), lambda h, i: (i, h)), pl.BlockSpec((None, TQ, 1), lambda h, i: (h, i, 0))],
        [jax.ShapeDtypeStruct((S, H * HEAD), BF16), jax.ShapeDtypeStruct((H, S, 1), F32)],
        [], (zb, zb, zb), ("arbitrary", "arbitrary"))


def attn_bwd(zb, ltot, dc, n_heads, jobs=()):
    S = zb.shape[0]
    TQ = _tile(S, ATTN_TQ)
    TK = min(ATTN_TK, TQ)
    R = TQ // TK
    scale = HEAD ** -0.5

    def body(q_ref, k_ref, v_ref, l_ref, do_ref, dq_ref, dk_ref, dv_ref):
        i = pl.program_id(1)

        @pl.when(i == 0)
        def _():
            dk_ref[...] = jnp.zeros_like(dk_ref)
            dv_ref[...] = jnp.zeros_like(dv_ref)

        q = q_ref[...]
        dob = do_ref[...].astype(BF16)
        ltot_rows = l_ref[...]
        earlier = _tri(TK, False)
        later = _tri(TK, True)

        def tile(kb, r0, c_lf, c_g, dq, mask):
            start = pl.multiple_of(kb * TK, TK)
            k = k_ref[pl.ds(start, TK), :]
            v = v_ref[pl.ds(start, TK), :]
            z = _dot_nt(q[r0:], k) * scale
            lf = _neg_softplus(z)
            if mask is not None:
                lf = jnp.where(mask, lf, 0.0)
            tot = c_lf + jnp.sum(lf, axis=1, keepdims=True)
            w = jnp.exp(z + lf + _dot(lf.astype(BF16), later) + (ltot_rows[r0:] - tot))
            if mask is not None:
                w = jnp.where(mask, w, 0.0)
            g = w * _dot_nt(dob[r0:], v)
            before = _dot(g.astype(BF16), earlier) + c_g
            sig = jnp.exp(z + lf)
            dz = g - sig * (g + before)
            if mask is not None:
                dz = jnp.where(mask, dz, 0.0)
            dzb = dz.astype(BF16)
            dv_ref[pl.ds(start, TK), :] += _dot_tn(w.astype(BF16), dob[r0:])
            dk_ref[pl.ds(start, TK), :] += scale * _dot_tn(dzb, q[r0:])
            dq = dq + scale * _dot(dzb, k)
            return (tot, c_g + jnp.sum(g, axis=1, keepdims=True), dq)

        zero = jnp.zeros((TQ, 1), F32)
        c_lf, c_g, dq = lax.fori_loop(0, i * R, lambda kb, c: tile(kb, 0, c[0], c[1], c[2], None),
                                      (zero, zero, jnp.zeros((TQ, HEAD), F32)))
        for r in range(R):
            r0 = r * TK
            c_r, g_r, dq_r = tile(i * R + r, r0, c_lf[r0:], c_g[r0:], dq[r0:], _causal(TQ - r0, TK))
            if r == 0:
                c_lf, c_g, dq = c_r, g_r, dq_r
            else:
                c_lf = jnp.concatenate([c_lf[:r0], c_r], axis=0)
                c_g = jnp.concatenate([c_g[:r0], g_r], axis=0)
                dq = jnp.concatenate([dq[:r0], dq_r], axis=0)
        dq_ref[...] = dq

    H = n_heads
    full = pl.BlockSpec((S, HEAD), lambda h, i: (0, h))
    blk = pl.BlockSpec((TQ, HEAD), lambda h, i: (i, h))
    shp = jax.ShapeDtypeStruct((S, H * HEAD), F32)
    return _carried_call(
        body, jobs, "attn_bwd", (H, S // TQ),
        [blk, pl.BlockSpec((S, HEAD), lambda h, i: (0, H + h)),
         pl.BlockSpec((S, HEAD), lambda h, i: (0, 2 * H + h)),
         pl.BlockSpec((None, TQ, 1), lambda h, i: (h, i, 0)), blk],
        [blk, full, full], [shp, shp, shp], [], (zb, zb, zb, ltot, dc), ("arbitrary", "arbitrary"))


def _rg_gates(u, ga, gx, lam):
    r = _sigmoid(ga)
    i = _sigmoid(gx)
    log_a = RG_C * r * _neg_softplus(-lam)
    a = jnp.exp(log_a)
    b = jnp.sqrt(-_expm1(2.0 * log_a)) * (i * u)
    return a, b


def _heads_matmul(u, w_ref, transpose):
    outs = []
    for hh in range(w_ref.shape[0]):
        uh = u[:, hh * HEAD:(hh + 1) * HEAD].astype(BF16)
        wv = w_ref[hh].astype(BF16)
        outs.append(_dot_nt(uh, wv) if transpose else _dot(uh, wv))
    return jnp.concatenate(outs, axis=1)


def _conv(buf, cw_ref, cb_ref, tb):
    u = cb_ref[...] + cw_ref[pl.ds(CONV_WIDTH - 1, 1), :] * buf[pl.ds(8, tb), :]
    for k in range(1, CONV_WIDTH):
        u = u + cw_ref[pl.ds(CONV_WIDTH - 1 - k, 1), :] * buf[pl.ds(8 - k, tb), :]
    return u


def _rnn_specs(tb, C, nb, col_gate, rev):
    blk_of = (lambda i: nb - 1 - i) if rev else (lambda i: i)
    blk = lambda col: pl.BlockSpec((tb, C), lambda i: (blk_of(i), col))
    halo = lambda col: pl.BlockSpec((8, C), lambda i: (jnp.maximum(blk_of(i) * (tb // 8) - 1, 0), col))
    return blk, halo


def rnn_fwd(z, cw, cb, wa, ba, wx, bx, lam):
    S = z.shape[0]
    C = cb.shape[-1]
    tb = _tile(S, 512)
    nb = S // tb

    def body(xg_ref, xr_ref, halo_ref, cw_ref, cb_ref, wa_ref, ba_ref, wx_ref, bx_ref, lam_ref,
             out_ref, hs_ref, buf, a_s, b_s, h_c):
        i = pl.program_id(0)

        @pl.when(i == 0)
        def _():
            h_c[...] = jnp.zeros_like(h_c)

        buf[pl.ds(0, 8), :] = jnp.where(i == 0, 0.0, halo_ref[...])
        buf[pl.ds(8, tb), :] = xr_ref[...]
        u = _conv(buf, cw_ref, cb_ref, tb)
        ga = _heads_matmul(u, wa_ref, False) + ba_ref[...]
        gx = _heads_matmul(u, wx_ref, False) + bx_ref[...]
        a, b = _rg_gates(u, ga, gx, lam_ref[...])
        a_s[...] = a
        b_s[...] = b

        def step(t, h):
            h = a_s[pl.ds(t, 1), :] * h + b_s[pl.ds(t, 1), :]
            hs_ref[pl.ds(t, 1), :] = h
            return h

        h_c[...] = lax.fori_loop(0, tb, step, h_c[...], unroll=8)
        out_ref[...] = (_gelu(xg_ref[...]) * hs_ref[...]).astype(BF16)

    blk, halo = _rnn_specs(tb, C, nb, 6, False)
    full = lambda a: pl.BlockSpec(a.shape, lambda i: (0,) * a.ndim)
    out = pl.BlockSpec((tb, C), lambda i: (i, 0))
    return pl.pallas_call(
        body, name="rnn_fwd", grid=(nb,),
        in_specs=[blk(6), blk(7), halo(7)] + [full(a) for a in (cw, cb, wa, ba, wx, bx, lam)],
        out_specs=[out, out],
        out_shape=[jax.ShapeDtypeStruct((S, C), BF16), jax.ShapeDtypeStruct((S, C), F32)],
        scratch_shapes=[pltpu.VMEM((tb + 8, C), F32), pltpu.VMEM((tb, C), F32), pltpu.VMEM((tb, C), F32),
                        pltpu.VMEM((1, C), F32)],
        compiler_params=_params("arbitrary"))(z, z, z, cw, cb, wa, ba, wx, bx, lam)


def rnn_bwd(z, hs, dc, cw, cb, wa, ba, wx, bx, lam):
    S = z.shape[0]
    C = cb.shape[-1]
    tb = _tile(S, 512)
    nb = S // tb

    def body(xg_ref, xr_ref, halo_ref, hs_ref, hs_halo_ref, do_ref,
             cw_ref, cb_ref, wa_ref, ba_ref, wx_ref, bx_ref, lam_ref,
             dxg_ref, dxr_ref, dcw_ref, dcb_ref, dwa_ref, dba_ref, dwx_ref, dbx_ref, dlam_ref,
             buf, hbuf, dubuf, a_s, dh_s, carry):
        i = pl.program_id(0)
        first = i == nb - 1

        @pl.when(i == 0)
        def _():
            carry[...] = jnp.zeros_like(carry)
            dubuf[pl.ds(tb, 8), :] = jnp.zeros((8, C), F32)
            for r in (dcw_ref, dcb_ref, dwa_ref, dba_ref, dwx_ref, dbx_ref, dlam_ref):
                r[...] = jnp.zeros_like(r)

        buf[pl.ds(0, 8), :] = jnp.where(first, 0.0, halo_ref[...])
        buf[pl.ds(8, tb), :] = xr_ref[...]
        u = _conv(buf, cw_ref, cb_ref, tb)
        ga = _heads_matmul(u, wa_ref, False) + ba_ref[...]
        gx = _heads_matmul(u, wx_ref, False) + bx_ref[...]
        (a, _), gates_vjp = jax.vjp(_rg_gates, u, ga, gx, lam_ref[...])
        gel, gelu_vjp = jax.vjp(_gelu, xg_ref[...])
        hs = hs_ref[...]
        do = do_ref[...]
        dxg_ref[...] = gelu_vjp(do * hs)[0].astype(BF16)
        dh_s[...] = do * gel
        a_s[...] = a

        def step(n, c):
            t = tb - 1 - n
            acc = dh_s[pl.ds(t, 1), :] + c
            dh_s[pl.ds(t, 1), :] = acc
            return a_s[pl.ds(t, 1), :] * acc

        carry[...] = lax.fori_loop(0, tb, step, carry[...], unroll=8)
        db = dh_s[...]
        hbuf[pl.ds(0, 8), :] = jnp.where(first, 0.0, hs_halo_ref[...])
        hbuf[pl.ds(8, tb), :] = hs
        da = db * hbuf[pl.ds(7, tb), :]
        du, dga, dgx, dlam = gates_vjp((da, db))
        du = du + _heads_matmul(dga, wa_ref, True) + _heads_matmul(dgx, wx_ref, True)
        dlam_ref[...] += dlam
        dba_ref[...] += jnp.sum(dga, axis=0, keepdims=True)
        dbx_ref[...] += jnp.sum(dgx, axis=0, keepdims=True)
        ub = u.astype(BF16)
        for hh in range(wa_ref.shape[0]):
            cols = slice(hh * HEAD, (hh + 1) * HEAD)
            dwa_ref[hh] += _dot_tn(ub[:, cols], dga[:, cols].astype(BF16))
            dwx_ref[hh] += _dot_tn(ub[:, cols], dgx[:, cols].astype(BF16))
        dcb_ref[...] += jnp.sum(du, axis=0, keepdims=True)
        dubuf[pl.ds(0, tb), :] = du
        dxr = cw_ref[pl.ds(CONV_WIDTH - 1, 1), :] * du
        dcw_ref[pl.ds(CONV_WIDTH - 1, 1), :] += jnp.sum(du * buf[pl.ds(8, tb), :], axis=0, keepdims=True)
        for k in range(1, CONV_WIDTH):
            dxr = dxr + cw_ref[pl.ds(CONV_WIDTH - 1 - k, 1), :] * dubuf[pl.ds(k, tb), :]
            dcw_ref[pl.ds(CONV_WIDTH - 1 - k, 1), :] += jnp.sum(du * buf[pl.ds(8 - k, tb), :], axis=0,
                                                                keepdims=True)
        dxr_ref[...] = dxr.astype(BF16)
        dubuf[pl.ds(tb, 8), :] = du[0:8, :]

    blk, halo = _rnn_specs(tb, C, nb, 6, True)
    full = lambda a: pl.BlockSpec(a.shape, lambda i: (0,) * a.ndim)
    out = pl.BlockSpec((tb, C), lambda i: (nb - 1 - i, 0))
    params = (cw, cb, wa, ba, wx, bx, lam)
    return pl.pallas_call(
        body, name="rnn_bwd", grid=(nb,),
        in_specs=[blk(6), blk(7), halo(7), blk(0), halo(0), blk(2)] + [full(a) for a in params],
        out_specs=[out, out] + [full(a) for a in params],
        out_shape=[jax.ShapeDtypeStruct((S, C), BF16)] * 2 + [jax.ShapeDtypeStruct(a.shape, F32) for a in params],
        scratch_shapes=[pltpu.VMEM((tb + 8, C), F32), pltpu.VMEM((tb + 8, C), F32), pltpu.VMEM((tb + 8, C), F32),
                        pltpu.VMEM((tb, C), F32), pltpu.VMEM((tb, C), F32), pltpu.VMEM((1, C), F32)],
        compiler_params=_params("arbitrary"))(z, z, z, hs, hs, dc, *params)


def _pool_stats(buf, x, t0, tb, n_groups):
    t = t0 + lax.broadcasted_iota(jnp.int32, (tb, 1), 0)
    ds, cnts = [], []
    for g in range(n_groups):
        win = POOL_WINDOWS[g]
        cols = slice(g * HEAD, (g + 1) * HEAD)
        s = buf[pl.ds(16, tb), cols]
        for k in range(1, win):
            s = s + buf[pl.ds(16 - k, tb), cols]
        cnt = jnp.minimum(t + 1, win).astype(F32)
        ds.append(s / cnt - x[:, cols])
        cnts.append(cnt)
    return ds, cnts


def _pool_specs(tb, C, nb, col, rev):
    blk_of = (lambda i: nb - 1 - i) if rev else (lambda i: i)
    blk = pl.BlockSpec((tb, C), lambda i: (blk_of(i), col))
    halo = pl.BlockSpec((16, C), lambda i: (jnp.maximum(blk_of(i) * (tb // 16) - 1, 0), col))
    return blk, halo


def pool_fwd(z, pw, ps):
    S = z.shape[0]
    C = ps.shape[-1]
    G = pw.shape[0]
    tb = _tile(S, 512)
    nb = S // tb

    def body(x_ref, halo_ref, pw_ref, ps_ref, out_ref, buf):
        i = pl.program_id(0)
        buf[pl.ds(0, 16), :] = jnp.where(i == 0, 0.0, halo_ref[...])
        x = x_ref[...]
        buf[pl.ds(16, tb), :] = x
        ds, _ = _pool_stats(buf, x, i * tb, tb, G)
        y = jnp.concatenate([_dot(ds[g].astype(BF16), pw_ref[g].astype(BF16)) for g in range(G)], axis=1)
        out_ref[...] = (y * ps_ref[...]).astype(BF16)

    blk, halo = _pool_specs(tb, C, nb, 8, False)
    full = lambda a: pl.BlockSpec(a.shape, lambda i: (0,) * a.ndim)
    return pl.pallas_call(
        body, name="pool_fwd", grid=(nb,), in_specs=[blk, halo, full(pw), full(ps)],
        out_specs=pl.BlockSpec((tb, C), lambda i: (i, 0)), out_shape=jax.ShapeDtypeStruct((S, C), BF16),
        scratch_shapes=[pltpu.VMEM((tb + 16, C), F32)], compiler_params=_params("arbitrary"))(z, z, pw, ps)


def pool_bwd(z, dc, pw, ps):
    S = z.shape[0]
    C = ps.shape[-1]
    G = pw.shape[0]
    tb = _tile(S, 512)
    nb = S // tb

    def body(x_ref, halo_ref, dy_ref, pw_ref, ps_ref, dx_ref, dpw_ref, dps_ref, buf, ebuf):
        i = pl.program_id(0)
        ib = nb - 1 - i

        @pl.when(i == 0)
        def _():
            ebuf[pl.ds(tb, 16), :] = jnp.zeros((16, C), F32)
            dpw_ref[...] = jnp.zeros_like(dpw_ref)
            dps_ref[...] = jnp.zeros_like(dps_ref)

        buf[pl.ds(0, 16), :] = jnp.where(ib == 0, 0.0, halo_ref[...])
        x = x_ref[...]
        buf[pl.ds(16, tb), :] = x
        ds, cnts = _pool_stats(buf, x, ib * tb, tb, G)
        dy = dy_ref[...]
        dyp = dy * ps_ref[...]
        ypre, dds, es = [], [], []
        for g in range(G):
            cols = slice(g * HEAD, (g + 1) * HEAD)
            db = ds[g].astype(BF16)
            wv = pw_ref[g].astype(BF16)
            dypb = dyp[:, cols].astype(BF16)
            ypre.append(_dot(db, wv))
            dd = _dot_nt(dypb, wv)
            dpw_ref[g] += _dot_tn(db, dypb)
            dds.append(dd)
            es.append(dd / cnts[g])
        dps_ref[...] += jnp.sum(dy * jnp.concatenate(ypre, axis=1), axis=0, keepdims=True)
        e = jnp.concatenate(es, axis=1)
        ebuf[pl.ds(0, tb), :] = e
        dxs = []
        for g in range(G):
            cols = slice(g * HEAD, (g + 1) * HEAD)
            s = es[g]
            for k in range(1, POOL_WINDOWS[g]):
                s = s + ebuf[pl.ds(k, tb), cols]
            dxs.append(s - dds[g])
        dx_ref[...] = jnp.concatenate(dxs, axis=1).astype(BF16)
        ebuf[pl.ds(tb, 16), :] = e[0:16, :]

    blk, halo = _pool_specs(tb, C, nb, 8, True)
    dyb, _ = _pool_specs(tb, C, nb, 3, True)
    full = lambda a: pl.BlockSpec(a.shape, lambda i: (0,) * a.ndim)
    return pl.pallas_call(
        body, name="pool_bwd", grid=(nb,), in_specs=[blk, halo, dyb, full(pw), full(ps)],
        out_specs=[pl.BlockSpec((tb, C), lambda i: (nb - 1 - i, 0)), full(pw), full(ps)],
        out_shape=[jax.ShapeDtypeStruct((S, C), BF16), jax.ShapeDtypeStruct(pw.shape, F32),
                   jax.ShapeDtypeStruct(ps.shape, F32)],
        scratch_shapes=[pltpu.VMEM((tb + 16, C), F32), pltpu.VMEM((tb + 16, C), F32)],
        compiler_params=_params("arbitrary"))(z, z, dc, pw, ps)


def adamw(w, g, m, v):
    R, N = w.shape
    tr = _tile(R, 256)

    def body(w_ref, g_ref, m_ref, v_ref, d_ref, m2_ref, v2_ref):
        gv = g_ref[...]
        m2 = ADAM_B1 * m_ref[...] + (1.0 - ADAM_B1) * gv
        v2 = ADAM_B2 * v_ref[...] + (1.0 - ADAM_B2) * (gv * gv)
        m_hat = m2 / (1.0 - ADAM_B1 ** ADAM_STEP)
        v_hat = v2 / (1.0 - ADAM_B2 ** ADAM_STEP)
        d_ref[...] = -ADAM_LR * (m_hat / (jnp.sqrt(v_hat) + ADAM_EPS) + ADAM_WD * w_ref[...])
        m2_ref[...] = m2
        v2_ref[...] = v2

    blk = pl.BlockSpec((tr, N), lambda i: (i, 0))
    shp = jax.ShapeDtypeStruct((R, N), F32)
    return pl.pallas_call(body, name="adamw", grid=(R // tr,), in_specs=[blk] * 4, out_specs=[blk] * 3,
                          out_shape=[shp] * 3, compiler_params=_params("arbitrary"))(w, g, m, v)


def own_slot(w, dtype, me_arr):
    L, K, N = w.shape
    tk = _tile(K, 512)

    def body(me_ref, w_ref, o_ref):
        o_ref[...] = w_ref[...].astype(dtype)

    return pl.pallas_call(
        body, name="own_slot",
        grid_spec=pltpu.PrefetchScalarGridSpec(
            num_scalar_prefetch=1, grid=(L, K // tk),
            in_specs=[pl.BlockSpec((None, tk, N), lambda l, i, me: (l, i, 0))],
            out_specs=pl.BlockSpec((None, None, tk, N), lambda l, i, me: (me[0], l, i, 0))),
        out_shape=jax.ShapeDtypeStruct((N_CHIP, L, K, N), dtype),
        compiler_params=_params("arbitrary", "arbitrary"))(me_arr, w)


def chip_sum(g, got, c_arr):
    _, K, N = g.shape
    Kh = K // 2
    tr = _tile(Kh, 352 if Kh % 352 == 0 else 256)

    def body(c_ref, g_ref, got_ref, o_ref):
        o_ref[...] = (g_ref[...] + got_ref[...]).astype(BF16)

    nt = Kh // tr
    return pl.pallas_call(
        body, name="chip_sum",
        grid_spec=pltpu.PrefetchScalarGridSpec(
            num_scalar_prefetch=1, grid=(N_CHIP, nt),
            in_specs=[pl.BlockSpec((None, tr, N), lambda j, i, c: (j, c[0] * nt + i, 0)),
                      pl.BlockSpec((None, tr, N), lambda j, i, c: (j, i, 0))],
            out_specs=pl.BlockSpec((None, tr, N), lambda j, i, c: (j, i, 0))),
        out_shape=jax.ShapeDtypeStruct((N_CHIP, Kh, N), BF16),
        compiler_params=_params("arbitrary", "arbitrary"))(c_arr, g, got)


def total_sum(cs, got, me_arr, c_arr, l, into):
    _, Kh, N = cs.shape
    tr = _tile(Kh, 352 if Kh % 352 == 0 else 256)
    nt = Kh // tr

    def body(me_ref, c_ref, cs_ref, got_ref, *rest):
        o_ref = rest[-1]
        o_ref[...] = ((cs_ref[...].astype(F32) + got_ref[0].astype(F32)) + got_ref[1].astype(F32)) + \
            got_ref[2].astype(F32)

    ins = [pl.BlockSpec((None, tr, N), lambda i, me, c: (me[0], i, 0)),
           pl.BlockSpec((3, tr, N), lambda i, me, c: (0, i, 0))]
    args = [me_arr, c_arr, cs, got]
    alias = {}
    if into is not None:
        ins.append(pl.BlockSpec(memory_space=pl.ANY))
        args.append(into)
        alias = {4: 0}
    return pl.pallas_call(
        body, name="total_sum",
        grid_spec=pltpu.PrefetchScalarGridSpec(
            num_scalar_prefetch=2, grid=(nt,), in_specs=ins,
            out_specs=pl.BlockSpec((None, tr, N), lambda i, me, c: (l, c[0] * nt + i, 0))),
        out_shape=jax.ShapeDtypeStruct((2, 2 * Kh, N), F32), input_output_aliases=alias,
        compiler_params=_params("arbitrary"))(*args)


def join_halves(buf):
    Kh = buf.shape[1] // 2

    def body(in_ref, out_ref, send_sem, recv_sem):
        x, y, c, _ = _place()
        mine = out_ref.at[:, pl.ds(c * Kh, Kh), :]
        cp = pltpu.make_async_remote_copy(
            src_ref=mine, dst_ref=mine, send_sem=send_sem, recv_sem=recv_sem,
            device_id=(x, y, 1 - c), device_id_type=MESH)
        cp.start()
        cp.wait()

    return pl.pallas_call(
        body, name="join_halves", in_specs=[HBM], out_specs=HBM,
        out_shape=jax.ShapeDtypeStruct(buf.shape, F32), input_output_aliases={0: 0},
        scratch_shapes=[pltpu.SemaphoreType.DMA, pltpu.SemaphoreType.DMA])(buf)


def allreduce_small(p):
    R = p.shape[0]

    def body(p_ref, out_ref, all_ref, send_sems, recv_sems, local_sem):
        x, y, c, others = _place()
        me, sibling = (x, y, c), (x, y, 1 - c)

        def rows(px, py, pc):
            return all_ref.at[pl.ds((4 * px + 2 * py + pc) * R, R), :]

        def copy(k, block, to, src=None):
            return pltpu.make_async_remote_copy(
                src_ref=rows(*block) if src is None else src, dst_ref=rows(*block), send_sem=send_sems.at[k],
                recv_sem=recv_sems.at[k], device_id=to, device_id_type=MESH)

        mine = pltpu.make_async_copy(p_ref, rows(*me), local_sem)
        mine.start()
        first = [copy(0, me, sibling, src=p_ref)]
        first += [copy(1 + j, me, (*chip, c), src=p_ref) for j, chip in enumerate(others)]
        for cp in first:
            cp.start()
        passed = [copy(4 + j, (*chip, c), sibling) for j, chip in enumerate(others)]
        for j, chip in enumerate(others):
            copy(1 + j, (*chip, c), me).wait_recv()
            passed[j].start()
        copy(0, sibling, me).wait_recv()
        for j, chip in enumerate(others):
            copy(4 + j, (*chip, 1 - c), me).wait_recv()
        for cp in first + passed:
            cp.wait_send()
        mine.wait()
        total = all_ref[pl.ds(0, R), :]
        for d in range(1, N_DEV):
            total = total + all_ref[pl.ds(d * R, R), :]
        out_ref[...] = total

    vmem = pl.BlockSpec(memory_space=pltpu.VMEM)
    return pl.pallas_call(
        body, name="allreduce_small", in_specs=[vmem], out_specs=vmem,
        out_shape=jax.ShapeDtypeStruct((R, 128), F32),
        scratch_shapes=[pltpu.VMEM((N_DEV * R, 128), F32), pltpu.SemaphoreType.DMA((7,)),
                        pltpu.SemaphoreType.DMA((7,)), pltpu.SemaphoreType.DMA],
        compiler_params=pltpu.CompilerParams(vmem_limit_bytes=VMEM_LIMIT_V7X))(p)


def _ffn_fwd(x, g, Wg, names, l, slots, late):
    h = rmsnorm_fwd(x, g)
    (a, b, y), got = ffn_up(h, Wg[names[0]], Wg[names[1]], l, [_gather_job(slots[n]) for n in late])
    for n, o in zip(late, got):
        Wg[n] = o[0]
    return residual_matmul(x, y, Wg[names[2]], l, 0.5, True), (x, h, a, b, y)


class _Reducer:
    def __init__(self, c_arr):
        self.c_arr, self.fresh, self.summed, self.landed = c_arr, [], [], {}

    def jobs(self):
        return [_swap_job(g) for _, _, g in self.fresh] + [_scatter_job(cs) for _, _, cs in self.summed]

    def done(self, got):
        n_swap = len(self.fresh)
        for (n, l, cs), o in zip(self.summed, got[n_swap:]):
            self.landed[n, l] = (cs, o[0])
        self.summed = [(n, l, chip_sum(g, o[0], self.c_arr)) for (n, l, g), o in zip(self.fresh, got[:n_swap])]
        self.fresh = []

    def weight_grad(self, name, l, *args, **kw):
        g, got = weight_grad(*args, jobs=self.jobs(), **kw)
        self.done(got)
        self.fresh.append((name, l, g))

    def flush(self):
        while self.fresh or self.summed:
            self.done(_run_jobs(self.jobs(), "reduce_tail"))


def _ffn_bwd(red, names, dxo, saved, g, Wg, l):
    x, h, a, b, y = saved
    D = x.shape[1]
    wg, wu, wd = (Wg[n] for n in names)
    F = wg.shape[-1]
    da, db = ffn_bwd_hidden(dxo, wd, a, b, l)
    red.weight_grad(names[2], l, y, dxo, F, D, True, False, b_scale=0.5)
    red.weight_grad(names[0], l, h, da, D, F, False, True)
    red.weight_grad(names[1], l, h, db, D, F, False, True)
    (dx, dg), got = norm_bwd_matmul(x, g, dxo, [(da, wg), (db, wu)], l, True, red.jobs())
    red.done(got)
    return dx, dg


def kernel(x, norm_ffn1, ffn1_gate, ffn1_up, ffn1_down, norm_mix, w_in, conv_w, conv_b, rg_w_a, rg_b_a, rg_w_x, rg_b_x, rg_lambda, pool_w, pool_scale, w_out, norm_ffn2, ffn2_gate, ffn2_up, ffn2_down, norm_final, loss_target, m_norm_ffn1, m_ffn1_gate, m_ffn1_up, m_ffn1_down, m_norm_mix, m_w_in, m_conv_w, m_conv_b, m_rg_w_a, m_rg_b_a, m_rg_w_x, m_rg_b_x, m_rg_lambda, m_pool_w, m_pool_scale, m_w_out, m_norm_ffn2, m_ffn2_gate, m_ffn2_up, m_ffn2_down, m_norm_final, v_norm_ffn1, v_ffn1_gate, v_ffn1_up, v_ffn1_down, v_norm_mix, v_w_in, v_conv_w, v_conv_b, v_rg_w_a, v_rg_b_a, v_rg_w_x, v_rg_b_x, v_rg_lambda, v_pool_w, v_pool_scale, v_w_out, v_norm_ffn2, v_ffn2_gate, v_ffn2_up, v_ffn2_down, v_norm_final):
    given = dict(locals())
    W = {n: given[n] for n in WEIGHTS}
    M = {n: given["m_" + n] for n in WEIGHTS}
    V = {n: given["v_" + n] for n in WEIGHTS}
    depth = norm_ffn1.shape[0]
    S, D = x.shape[1], x.shape[2]
    xs = x.reshape(S, D)
    target = loss_target.reshape(S, D)
    C = conv_b.shape[-1]
    n_heads = (D // 2) // HEAD
    cx, cy, cc = lax.axis_index("x"), lax.axis_index("y"), lax.axis_index("c")
    me = 2 * cx + cy
    me_arr = jnp.reshape(me, (1,)).astype(jnp.int32)
    c_arr = jnp.reshape(cc, (1,)).astype(jnp.int32)

    slots = {n: own_slot(W[n], BF16, me_arr) for n in BIG}
    Wg = {n: _run_jobs([_gather_job(slots[n])], "gather_weight")[0][0] for n in ('ffn1_gate', 'ffn1_up')}
    conv_all = _run_jobs([_gather_job(own_slot(conv_w, F32, me_arr))], "gather_weight")[0][0]
    conv_full = jnp.transpose(conv_all, (1, 2, 0, 3)).reshape(depth, CONV_WIDTH, C)

    row = lambda a, l: a[l].reshape(1, -1)

    saved = []
    cur = xs
    for l in range(depth):
        cur, s1 = _ffn_fwd(cur, row(norm_ffn1, l), Wg, ('ffn1_gate', 'ffn1_up', 'ffn1_down'), l, slots,
                           ['ffn1_down', 'w_in'] if l == 0 else [])
        x1 = cur
        hm = rmsnorm_fwd(x1, row(norm_mix, l))
        z, zb = mix_in(hm, Wg['w_in'], l)
        late = ['w_out', 'ffn2_gate', 'ffn2_up', 'ffn2_down'] if l == 0 else []
        (att, ltot), got = attn_fwd(zb, n_heads, [_gather_job(slots[n]) for n in late])
        for n, o in zip(late, got):
            Wg[n] = o[0]
        rnn_p = (conv_full[l], row(conv_b, l), rg_w_a[l], row(rg_b_a, l), rg_w_x[l], row(rg_b_x, l),
                 row(rg_lambda, l))
        rnn, hs = rnn_fwd(z, *rnn_p)
        pool = pool_fwd(z, pool_w[l], row(pool_scale, l))
        cat = jnp.concatenate([att, rnn, pool], axis=1)
        cur = residual_matmul(x1, cat, Wg['w_out'], l, 1.0, False)
        cur, s2 = _ffn_fwd(cur, row(norm_ffn2, l), Wg, ('ffn2_gate', 'ffn2_up', 'ffn2_down'), l, slots, [])
        saved.append((s1, (x1, hm, z, zb, ltot, hs, cat, rnn_p), s2))

    loss_cols, dx, g_norm_final = final_loss(cur, norm_final.reshape(1, D), target)

    G = {n: [None] * depth for n in SMALL if n != 'norm_final'}
    red = _Reducer(c_arr)
    for l in reversed(range(depth)):
        s1, (x1, hm, z, zb, ltot, hs, cat, rnn_p), s2 = saved[l]
        dx, G['norm_ffn2'][l] = _ffn_bwd(red, ('ffn2_gate', 'ffn2_up', 'ffn2_down'), dx, s2, row(norm_ffn2, l), Wg, l)
        dc = mix_out_bwd(dx, Wg['w_out'], l)
        red.weight_grad('w_out', l, cat, dx, C, D, False, False)
        (dq, dk, dv), got = attn_bwd(zb, ltot, dc, n_heads, red.jobs())
        red.done(got)
        (dxg, dxr, G['conv_w'][l], G['conv_b'][l], G['rg_w_a'][l], G['rg_b_a'][l], G['rg_w_x'][l],
         G['rg_b_x'][l], G['rg_lambda'][l]) = rnn_bwd(z, hs, dc, *rnn_p)
        dxp, G['pool_w'][l], G['pool_scale'][l] = pool_bwd(z, dc, pool_w[l], row(pool_scale, l))
        dz = jnp.concatenate([dq.astype(BF16), dk.astype(BF16), dv.astype(BF16), dxg, dxr, dxp], axis=1)
        red.weight_grad('w_in', l, hm, dz, D, dz.shape[1] // N_CHIP, False, False)
        (dx, G['norm_mix'][l]), got = norm_bwd_matmul(x1, row(norm_mix, l), dx, [(dz, Wg['w_in'])], l, False,
                                                      red.jobs())
        red.done(got)
        dx, G['norm_ffn1'][l] = _ffn_bwd(red, ('ffn1_gate', 'ffn1_up', 'ffn1_down'), dx, s1, row(norm_ffn1, l), Wg, l)
    grad_x = dx.reshape(x.shape)
    red.flush()

    outs = {}
    for n in BIG:
        halves = None
        for l in range(depth):
            cs, got = red.landed[n, l]
            halves = total_sum(cs, got, me_arr, c_arr, l, halves)
        g_red = join_halves(halves)
        two_d = (-1, g_red.shape[-1])
        d, m2, v2 = adamw(W[n].reshape(two_d), g_red.reshape(two_d), M[n].reshape(two_d), V[n].reshape(two_d))
        outs[n] = (g_red, d.reshape(W[n].shape), m2.reshape(W[n].shape), v2.reshape(W[n].shape))

    loss_part = 0.5 * jnp.sum(loss_cols) / D
    parts = [jnp.stack(G[n]).reshape(-1) for n in SMALL if n != 'norm_final']
    parts += [g_norm_final.reshape(-1), jnp.reshape(loss_part, (1,))]
    sizes = [p.shape[0] for p in parts]
    total = sum(sizes)
    R = -(-total // 1024) * 8
    packed = jnp.concatenate(parts + [jnp.zeros((R * 128 - total,), F32)]).reshape(R, 128)
    red = allreduce_small(packed).reshape(-1)
    offs = [sum(sizes[:i]) for i in range(len(sizes))]
    small_names = [n for n in SMALL if n != 'norm_final'] + ['norm_final']
    small_g = {}
    for n, o, sz in zip(small_names, offs, sizes):
        if n == 'conv_w':
            full = red[o:o + sz].reshape(depth, CONV_WIDTH, C)
            small_g[n] = lax.dynamic_slice_in_dim(full, me * (C // N_CHIP), C // N_CHIP, axis=2)
        else:
            small_g[n] = red[o:o + sz].reshape(W[n].shape)
    loss = red[offs[-1]]

    def pack(d):
        flat = jnp.concatenate([d[n].reshape(-1) for n in small_names])
        rows = -(-flat.shape[0] // 1024) * 8
        return jnp.concatenate([flat, jnp.ones((rows * 128 - flat.shape[0],), F32)]).reshape(rows, 128)

    d_s, m_s, v_s = adamw(pack(W), pack(small_g), pack(M), pack(V))
    o = 0
    for n in small_names:
        sz = W[n].size
        outs[n] = (small_g[n],) + tuple(a.reshape(-1)[o:o + sz].reshape(W[n].shape) for a in (d_s, m_s, v_s))
        o += sz

    return (loss, grad_x, *[outs[n][0] for n in WEIGHTS], *[outs[n][1] for n in WEIGHTS],
            *[outs[n][2] for n in WEIGHTS], *[outs[n][3] for n in WEIGHTS])
```
